```python
import jax, jax.numpy as jnp
from jax import lax
import numpy as np

D_MODEL = 2048
BATCH = 2
SEQ = 16384
DEPTH = 4
DEC_BATCH = 8
DEC_SEQ = 32
PAST_LEN = 1024

CHUNK = 64
D_MIX = D_MODEL // 2
HEAD_DIM = 128
D_SB = D_MIX // 2
D_FOX = D_MIX - D_SB
H_SB = D_SB // HEAD_DIM
H_FOX = D_FOX // HEAD_DIM
PLE_DIM = 256
Q_BLOCK = 128
K_BLOCK = 128
SUPER_BLOCK = 1024
EPS = 1e-6
NEG_BIG = -1e30
FORGET_BIAS = 3.0
D_IN = 4 * D_SB + 4 * D_FOX + H_FOX

kernel_name = "hymba_stickbreak_fox_streaming_step"


def rms_norm(x, g):
    xf = x.astype(jnp.float32)
    var = jnp.mean(xf * xf, axis=-1, keepdims=True)
    return (xf * lax.rsqrt(var + EPS)).astype(x.dtype) * g


def head_rms_norm(o, g, n_heads):
    B, T, _ = o.shape
    of = o.reshape(B, T, n_heads, HEAD_DIM).astype(jnp.float32)
    of = of * lax.rsqrt(jnp.mean(of * of, axis=-1, keepdims=True) + EPS)
    return of.reshape(B, T, n_heads * HEAD_DIM).astype(o.dtype) * g


def map_query_blocks(fn, *qs):
    T = qs[0].shape[1]
    if T <= Q_BLOCK:
        return fn(*qs)
    nb = T // Q_BLOCK

    def to_blocks(a):
        a = a.reshape(a.shape[0], nb, Q_BLOCK, *a.shape[2:])
        return jnp.moveaxis(a, 1, 0)

    out = lax.map(lambda blk: fn(*blk), tuple(to_blocks(a) for a in qs))
    out = jnp.moveaxis(out, 0, 1)
    return out.reshape(out.shape[0], T, *out.shape[3:])


def causal_sweep(block_fn, q_arrays, P, T):
    outs = []
    for start in range(0, T, SUPER_BLOCK):
        end = min(start + SUPER_BLOCK, T)
        n_keys = -(-(P + end) // K_BLOCK) * K_BLOCK
        qs = [a[:, start:end] for a in q_arrays]
        outs.append(map_query_blocks(lambda *b, n=n_keys: block_fn(n, *b), *qs))
    return outs[0] if len(outs) == 1 else jnp.concatenate(outs, axis=1)


def stick_breaking_attention(q, k, v, q_pos, k_pos, P, T):
    scale = HEAD_DIM ** -0.5
    after_in_block = jnp.tril(jnp.ones((K_BLOCK, K_BLOCK), jnp.float32), -1)

    def block(n_keys, qb, pb):
        kb, vb, kp = k[:, :n_keys], v[:, :n_keys], k_pos[:n_keys]
        nk = n_keys // K_BLOCK
        z = jnp.einsum("bqhd,bkhd->bhqk", qb, kb).astype(jnp.float32) * scale
        allowed = kp[None, None, None, :] < pb[:, None, :, None]
        log_1m_beta = jnp.where(allowed, jax.nn.log_sigmoid(-z), 0.0)
        Bq, Hq, Qq, _ = z.shape
        lb = log_1m_beta.reshape(Bq, Hq, Qq, nk, K_BLOCK)
        inner = jnp.einsum("bhqnp,pj->bhqnj", lb, after_in_block, precision=lax.Precision.HIGHEST)
        after_blocks = jnp.tril(jnp.ones((nk, nk), jnp.float32), -1)
        cross = jnp.einsum("bhqm,mn->bhqn", lb.sum(-1), after_blocks, precision=lax.Precision.HIGHEST)
        suffix = (inner + cross[..., None]).reshape(Bq, Hq, Qq, n_keys)
        w = jnp.where(allowed, jnp.exp(jax.nn.log_sigmoid(z) + suffix), 0.0)
        return jnp.einsum("bhqk,bkhd->bqhd", w.astype(vb.dtype), vb)

    return causal_sweep(block, [q, q_pos], P, T)


def forgetting_attention(q, k, v, q_cum, k_cum, q_pos, k_pos, P, T):
    scale = HEAD_DIM ** -0.5
    k_cum_h = jnp.transpose(k_cum, (0, 2, 1))

    def block(n_keys, qb, cb, pb):
        kb, vb, kp = k[:, :n_keys], v[:, :n_keys], k_pos[:n_keys]
        s = jnp.einsum("bqhd,bkhd->bhqk", qb, kb).astype(jnp.float32) * scale
        bias = jnp.transpose(cb, (0, 2, 1))[..., None] - k_cum_h[:, :, None, :n_keys]
        allowed = kp[None, None, None, :] <= pb[:, None, :, None]
        w = jax.nn.softmax(jnp.where(allowed, s + bias, NEG_BIG), axis=-1)
        return jnp.einsum("bhqk,bkhd->bqhd", w.astype(vb.dtype), vb)

    return causal_sweep(block, [q, q_cum, q_pos], P, T)


def trunk_layer(x, p_i, w_in, b_f, g_attn, g_osb, g_ofox, w_out, w_ple, g_ple, w_ple_gate, past):
    B, T, _ = x.shape
    h = rms_norm(x, g_attn)
    proj = h @ w_in
    sizes = [D_SB] * 4 + [D_FOX] * 4
    idx, acc = [], 0
    for s_ in sizes:
        acc += s_
        idx.append(acc)
    q_sb, k_sb, v_sb, z_sb, q_fx, k_fx, v_fx, z_fx, f_fx = jnp.split(proj, idx, axis=-1)
    q_sb = q_sb.reshape(B, T, H_SB, HEAD_DIM)
    k_sb = k_sb.reshape(B, T, H_SB, HEAD_DIM)
    v_sb = v_sb.reshape(B, T, H_SB, HEAD_DIM)
    q_fx = q_fx.reshape(B, T, H_FOX, HEAD_DIM)
    k_fx = k_fx.reshape(B, T, H_FOX, HEAD_DIM)
    v_fx = v_fx.reshape(B, T, H_FOX, HEAD_DIM)
    logf = jax.nn.log_sigmoid((f_fx + b_f).astype(jnp.float32))

    if past is None:
        P = 0
        ks_all, vs_all, kf_all, vf_all, logf_all = k_sb, v_sb, k_fx, v_fx, logf
    else:
        c_ksb, c_vsb, c_kfx, c_vfx, c_logf = past
        P = c_ksb.shape[1]
        ks_all = jnp.concatenate([c_ksb, k_sb], axis=1)
        vs_all = jnp.concatenate([c_vsb, v_sb], axis=1)
        kf_all = jnp.concatenate([c_kfx, k_fx], axis=1)
        vf_all = jnp.concatenate([c_vfx, v_fx], axis=1)
        logf_all = jnp.concatenate([c_logf.astype(jnp.float32), logf], axis=1)
    cum_all = lax.cumsum(logf_all, axis=1)
    q_cum = cum_all[:, P:]
    L = P + T
    n_pad = -(-L // K_BLOCK) * K_BLOCK - L

    def pad_rows(a):
        if n_pad == 0:
            return a
        return jnp.pad(a, ((0, 0), (0, n_pad)) + ((0, 0),) * (a.ndim - 2))

    k_pos = jnp.arange(L + n_pad, dtype=jnp.int32)
    q_pos = (P + jnp.arange(T, dtype=jnp.int32))[None, :]

    o_sb = stick_breaking_attention(q_sb, pad_rows(ks_all), pad_rows(vs_all), q_pos, k_pos, P, T)
    o_fx = forgetting_attention(q_fx, pad_rows(kf_all), pad_rows(vf_all), q_cum, pad_rows(cum_all),
                                q_pos, k_pos, P, T)
    o_sb = head_rms_norm(o_sb.reshape(B, T, D_SB), g_osb, H_SB) * jax.nn.silu(z_sb)
    o_fx = head_rms_norm(o_fx.reshape(B, T, D_FOX), g_ofox, H_FOX) * jax.nn.silu(z_fx)
    x = x + jnp.concatenate([o_sb, o_fx], axis=-1) @ w_out
    x = x + (p_i @ w_ple) * jax.nn.sigmoid(rms_norm(x, g_ple) @ w_ple_gate)
    return x, (k_sb, v_sb, k_fx, v_fx, logf.astype(x.dtype))


def setup_inputs(seed: int = 0) -> dict:
    key = jax.random.key(seed)
    ks = jax.random.split(key, 20)
    f32 = jnp.float32
    nrm = lambda k, shp, s=1.0: jax.random.normal(k, shp, f32) * s
    return {
        "x_prompt": nrm(ks[0], (BATCH, SEQ, D_MODEL)),
        "x_sample": nrm(ks[1], (DEC_BATCH, DEC_SEQ, D_MODEL)),
        "p_prompt": nrm(ks[2], (DEPTH, BATCH, SEQ, PLE_DIM)),
        "p_sample": nrm(ks[3], (DEPTH, DEC_BATCH, DEC_SEQ, PLE_DIM)),
        "cache_sb_k": nrm(ks[4], (DEPTH, DEC_BATCH, PAST_LEN, H_SB, HEAD_DIM)),
        "cache_sb_v": nrm(ks[5], (DEPTH, DEC_BATCH, PAST_LEN, H_SB, HEAD_DIM)),
        "cache_fox_k": nrm(ks[6], (DEPTH, DEC_BATCH, PAST_LEN, H_FOX, HEAD_DIM)),
        "cache_fox_v": nrm(ks[7], (DEPTH, DEC_BATCH, PAST_LEN, H_FOX, HEAD_DIM)),
        "cache_fox_logf": jax.nn.log_sigmoid(FORGET_BIAS + nrm(ks[8], (DEPTH, DEC_BATCH, PAST_LEN, H_FOX))),
        "w_in": nrm(ks[9], (DEPTH, D_MODEL, D_IN), D_MODEL ** -0.5),
        "b_forget": FORGET_BIAS + nrm(ks[10], (DEPTH, H_FOX), 0.5),
        "g_attn_norm": 1.0 + nrm(ks[11], (DEPTH, D_MODEL), 0.02),
        "g_out_sb": 1.0 + nrm(ks[12], (DEPTH, D_SB), 0.02),
        "g_out_fox": 1.0 + nrm(ks[13], (DEPTH, D_FOX), 0.02),
        "w_out": nrm(ks[14], (DEPTH, D_MIX, D_MODEL), D_MIX ** -0.5),
        "w_ple": nrm(ks[15], (DEPTH, PLE_DIM, D_MODEL), PLE_DIM ** -0.5),
        "g_ple_norm": 1.0 + nrm(ks[16], (DEPTH, D_MODEL), 0.02),
        "w_ple_gate": nrm(ks[17], (DEPTH, D_MODEL, D_MODEL), D_MODEL ** -0.5),
        "g_final": 1.0 + nrm(ks[18], (D_MODEL,), 0.02),
    }


def reference(x_prompt, x_sample, p_prompt, p_sample, cache_sb_k, cache_sb_v, cache_fox_k,
              cache_fox_v, cache_fox_logf, w_in, b_forget, g_attn_norm, g_out_sb, g_out_fox,
              w_out, w_ple, g_ple_norm, w_ple_gate, g_final):
    xp, xs = x_prompt, x_sample
    new_p = [[] for _ in range(5)]
    new_s = [[] for _ in range(5)]
    for i in range(DEPTH):
        lw = (w_in[i], b_forget[i], g_attn_norm[i], g_out_sb[i], g_out_fox[i], w_out[i],
              w_ple[i], g_ple_norm[i], w_ple_gate[i])
        xp, st_p = trunk_layer(xp, p_prompt[i], *lw, None)
        past = (cache_sb_k[i], cache_sb_v[i], cache_fox_k[i], cache_fox_v[i], cache_fox_logf[i])
        xs, st_s = trunk_layer(xs, p_sample[i], *lw, past)
        for j in range(5):
            new_p[j].append(st_p[j])
            new_s[j].append(st_s[j])
    y_prompt = rms_norm(xp, g_final)
    y_sample = rms_norm(xs, g_final)
    sb_k_prompt, sb_v_prompt, fox_k_prompt, fox_v_prompt, fox_logf_prompt = [jnp.stack(a, axis=0) for a in new_p]
    sb_k_sample, sb_v_sample, fox_k_sample, fox_v_sample, fox_logf_sample = [jnp.stack(a, axis=0) for a in new_s]
    return (y_prompt, y_sample, sb_k_prompt, sb_v_prompt, fox_k_prompt, fox_v_prompt, fox_logf_prompt,
            sb_k_sample, sb_v_sample, fox_k_sample, fox_v_sample, fox_logf_sample)
```

```python
import functools
import math

import jax
import jax.numpy as jnp
from jax import lax
from jax.experimental import pallas as pl
from jax.experimental.pallas import tpu as pltpu

F32 = jnp.float32
BF16 = jnp.bfloat16

HEAD_DIM = 128
EPS = 1e-6
NEG_BIG = -1e30
LOG2E = math.log2(math.e)
ATTN_SCALE = HEAD_DIM ** -0.5

LANES_V7X = 128
VMEM_LIMIT_BYTES_V7X = 56 * 1024 * 1024

SB_UNDERFLOW_LOG = 110.0

ROW_TILE = 256
Q_TILE = 256
K_TILE = 256


def _cparams(n_axes):
    return pltpu.CompilerParams(
        dimension_semantics=("arbitrary",) * n_axes,
        vmem_limit_bytes=VMEM_LIMIT_BYTES_V7X,
    )


def _resident(block_shape, index_map):
    return pl.BlockSpec(block_shape, index_map, pipeline_mode=pl.Buffered(1))


def _rms_rows(x, gain):
    var = jnp.mean(x * x, axis=-1, keepdims=True)
    return (x * lax.rsqrt(var + EPS)) * gain


def _silu(z):
    return z * (1.0 / (1.0 + jnp.exp(-z)))


def _log_sigmoid(y):
    return jnp.minimum(y, 0.0) - jnp.log(1.0 + jnp.exp(-jnp.abs(y)))


def _dot_nt(a, b):
    return lax.dot_general(a, b, (((1,), (1,)), ((), ())), preferred_element_type=F32)


def _dot(a, b):
    return jnp.dot(a, b, preferred_element_type=F32)


def _in_proj_kernel(d_grp, x_ref, g_ref, w_ref, bf_ref, *refs):
    outs = refs[-13:]
    (qs_ref, ks_ref, vs_ref, ksb_ref, vsb_ref, gs_ref,
     qf_ref, kf_ref, vf_ref, kfb_ref, vfb_ref, gf_ref, lf_ref) = outs
    hb = _rms_rows(x_ref[...], g_ref[...]).astype(BF16)

    def proj(col, width):
        return _dot(hb, w_ref[:, col * d_grp: col * d_grp + width])

    qs_ref[...] = (proj(0, d_grp) * ATTN_SCALE).astype(BF16)
    k = proj(1, d_grp)
    ks_ref[...] = k
    ksb_ref[...] = k.astype(BF16)
    v = proj(2, d_grp)
    vs_ref[...] = v
    vsb_ref[...] = v.astype(BF16)
    gs_ref[...] = _silu(proj(3, d_grp))
    qf_ref[...] = (proj(4, d_grp) * (ATTN_SCALE * LOG2E)).astype(BF16)
    k = proj(5, d_grp)
    kf_ref[...] = k
    kfb_ref[...] = k.astype(BF16)
    v = proj(6, d_grp)
    vf_ref[...] = v
    vfb_ref[...] = v.astype(BF16)
    gf_ref[...] = _silu(proj(7, d_grp))
    lf_ref[...] = _log_sigmoid(proj(8, LANES_V7X) + bf_ref[...])


def _in_proj(x, g, w, bf, layer, depth, slabs):
    n, d_model = x.shape
    d_grp = (w.shape[1] - LANES_V7X) // 8
    tm = min(ROW_TILE, n)
    assert n % tm == 0
    row = lambda i: (i, 0)
    fixed = lambda i: (0, 0)
    slab_spec = pl.BlockSpec((None, tm, d_grp), lambda i: (layer, i, 0))
    grp_f32 = jax.ShapeDtypeStruct((n, d_grp), F32)
    grp_bf16 = jax.ShapeDtypeStruct((n, d_grp), BF16)
    slab = jax.ShapeDtypeStruct((depth, n, d_grp), F32)
    grp_spec = pl.BlockSpec((tm, d_grp), row)
    out_shape = (grp_bf16, slab, slab, grp_bf16, grp_bf16, grp_f32,
                 grp_bf16, slab, slab, grp_bf16, grp_bf16, grp_f32,
                 jax.ShapeDtypeStruct((n, LANES_V7X), F32))
    out_specs = (grp_spec, slab_spec, slab_spec, grp_spec, grp_spec, grp_spec,
                 grp_spec, slab_spec, slab_spec, grp_spec, grp_spec, grp_spec,
                 pl.BlockSpec((tm, LANES_V7X), row))
    in_specs = [pl.BlockSpec((tm, d_model), row), _resident((1, d_model), fixed),
                _resident(w.shape, fixed), _resident((1, LANES_V7X), fixed)]
    args = [x, g, w, bf]
    aliases = {}
    if slabs is not None:
        in_specs += [pl.BlockSpec(memory_space=pl.ANY)] * 4
        args += list(slabs)
        aliases = {4: 1, 5: 2, 6: 7, 7: 8}
    return pl.pallas_call(
        functools.partial(_in_proj_kernel, d_grp),
        grid=(n // tm,),
        in_specs=in_specs,
        out_specs=out_specs,
        out_shape=out_shape,
        input_output_aliases=aliases,
        compiler_params=_cparams(1),
        name="in_proj",
    )(*args)


def _cumsum_kernel(x_ref, o_ref):
    x = x_ref[...]
    length = x.shape[1]
    lane = lax.broadcasted_iota(jnp.int32, x.shape, 1)
    shift = 1
    while shift < length:
        x = x + jnp.where(lane >= shift, pltpu.roll(x, shift, axis=1), 0.0)
        shift *= 2
    o_ref[...] = x * LOG2E


def _cumsum_lanes(x):
    return pl.pallas_call(
        _cumsum_kernel,
        out_shape=jax.ShapeDtypeStruct(x.shape, F32),
        compiler_params=pltpu.CompilerParams(vmem_limit_bytes=VMEM_LIMIT_BYTES_V7X),
        name="cumsum",
    )(x)


def _suffix_ones(tk):
    p = lax.broadcasted_iota(jnp.int32, (tk, tk), 0)
    c = lax.broadcasted_iota(jnp.int32, (tk, tk), 1)
    return jnp.where(p >= c, 1.0, 0.0).astype(BF16)


def _sb_tile(q, kt, vt, run, tri, mask):
    z = _dot_nt(q, kt)
    sp = jnp.maximum(z, 0.0) + jnp.log(1.0 + jnp.exp(-jnp.abs(z)))
    if mask is not None:
        sp = jnp.where(mask, sp, 0.0)
    hi = sp.astype(BF16)
    lo = (sp - hi.astype(F32)).astype(BF16)
    cum = _dot(hi, tri) + _dot(lo, tri)
    w = jnp.exp(z - cum - run)
    if mask is not None:
        w = jnp.where(mask, w, 0.0)
    return _dot(w.astype(BF16), vt), run + cum[:, 0:1]


def _fox_tile(q, kt, vt, ck, cq, m, l, acc, mask):
    u = _dot_nt(q, kt) - ck
    if mask is not None:
        u = jnp.where(mask, u, NEG_BIG)
    m_new = jnp.maximum(m, jnp.max(u, axis=1, keepdims=True) + cq)
    p = jnp.exp2(u - (m_new - cq))
    alpha = jnp.exp2(m - m_new)
    l = alpha * l + jnp.sum(p, axis=1, keepdims=True)
    acc = alpha * acc + _dot(p.astype(BF16), vt)
    return m_new, l, acc


def _head_norm_gate(o, gain, gate):
    ms = jnp.mean(o * o, axis=-1, keepdims=True)
    return ((o * lax.rsqrt(ms + EPS)) * gain * gate).astype(BF16)


def _tile_iotas(tq, tk):
    return (lax.broadcasted_iota(jnp.int32, (tq, tk), 0), lax.broadcasted_iota(jnp.int32, (tq, tk), 1))


def _sb_prompt_kernel(q_ref, k_ref, v_ref, gate_ref, g_ref, o_ref):
    tq = q_ref.shape[0]
    i = pl.program_id(2)
    q = q_ref[...]
    tri = _suffix_ones(tq)
    rows, cols = _tile_iotas(tq, tq)
    start = pl.multiple_of(i * tq, tq)
    acc, run = _sb_tile(q, k_ref[pl.ds(start, tq), :], v_ref[pl.ds(start, tq), :],
                        jnp.zeros((tq, 1), F32), tri, cols < rows)

    def more(carry):
        j, _, _, run_min = carry
        return jnp.logical_and(j >= 0, run_min < SB_UNDERFLOW_LOG)

    def step(carry):
        j, acc, run, _ = carry
        s = pl.multiple_of(j * tq, tq)
        pv, run = _sb_tile(q, k_ref[pl.ds(s, tq), :], v_ref[pl.ds(s, tq), :], run, tri, None)
        return j - 1, acc + pv, run, jnp.min(run)

    _, acc, _, _ = lax.while_loop(more, step, (i - 1, acc, run, jnp.min(run)))
    o_ref[...] = _head_norm_gate(acc, g_ref[...], gate_ref[...])


def _fox_prompt_kernel(q_ref, k_ref, v_ref, c_ref, gate_ref, g_ref, o_ref):
    tq = q_ref.shape[0]
    i = pl.program_id(2)
    q = q_ref[...]
    rows, cols = _tile_iotas(tq, tq)
    start = pl.multiple_of(i * tq, tq)
    ck = c_ref[:, pl.ds(start, tq)]
    cq = jnp.sum(jnp.where(rows == cols, ck, 0.0), axis=1, keepdims=True)
    m, l, acc = _fox_tile(q, k_ref[pl.ds(start, tq), :], v_ref[pl.ds(start, tq), :], ck, cq,
                          jnp.full((tq, 1), NEG_BIG, F32), jnp.zeros((tq, 1), F32),
                          jnp.zeros((tq, HEAD_DIM), F32), cols <= rows)

    def step(j, carry):
        s = pl.multiple_of(j * tq, tq)
        return _fox_tile(q, k_ref[pl.ds(s, tq), :], v_ref[pl.ds(s, tq), :], c_ref[:, pl.ds(s, tq)], cq,
                         *carry, None)

    m, l, acc = lax.fori_loop(0, i, step, (m, l, acc))
    o_ref[...] = _head_norm_gate(acc / l, g_ref[...], gate_ref[...])


def _prompt_attention(kernel, q, k, v, gate, gain, cum=None):
    b, t, d_grp = q.shape
    n_heads = d_grp // HEAD_DIM
    tq = min(Q_TILE, t)
    assert t % tq == 0
    q_spec = pl.BlockSpec((None, tq, HEAD_DIM), lambda bi, h, i: (bi, i, h))
    kv_spec = pl.BlockSpec((None, t, HEAD_DIM), lambda bi, h, i: (bi, 0, h))
    in_specs = [q_spec, kv_spec, kv_spec]
    args = [q, k, v]
    if cum is not None:
        in_specs.append(pl.BlockSpec((None, 1, t), lambda bi, h, i: (bi * n_heads + h, 0, 0)))
        args.append(cum)
    in_specs += [q_spec, pl.BlockSpec((1, HEAD_DIM), lambda bi, h, i: (0, h))]
    args += [gate, gain]
    return pl.pallas_call(
        kernel,
        grid=(b, n_heads, t // tq),
        in_specs=in_specs,
        out_specs=q_spec,
        out_shape=jax.ShapeDtypeStruct((b, t, d_grp), BF16),
        compiler_params=_cparams(3),
        name=kernel.__name__.strip("_"),
    )(*args)


def _sb_sample_kernel(q_ref, kn_ref, vn_ref, kc_ref, vc_ref, gate_ref, g_ref, o_ref):
    tq, tn = q_ref.shape[0], kn_ref.shape[0]
    past = kc_ref.shape[0]
    tk = min(K_TILE, past)
    q = q_ref[...]
    rows, cols = _tile_iotas(tq, tn)
    acc, run = _sb_tile(q, kn_ref[...], vn_ref[...], jnp.zeros((tq, 1), F32), _suffix_ones(tn), cols < rows)
    tri = _suffix_ones(tk)
    for s in range(past - tk, -1, -tk):
        pv, run = _sb_tile(q, kc_ref[s:s + tk, :].astype(BF16), vc_ref[s:s + tk, :].astype(BF16), run, tri, None)
        acc = acc + pv
    o_ref[...] = _head_norm_gate(acc, g_ref[...], gate_ref[...])


def _fox_sample_kernel(q_ref, kn_ref, vn_ref, kc_ref, vc_ref, c_ref, gate_ref, g_ref, o_ref):
    tq, tn = q_ref.shape[0], kn_ref.shape[0]
    past = kc_ref.shape[0]
    q = q_ref[...]
    rows, cols = _tile_iotas(tq, tn)
    ck = c_ref[:, past:past + tn]
    cq = jnp.sum(jnp.where(rows == cols, ck, 0.0), axis=1, keepdims=True)
    carry = _fox_tile(q, kn_ref[...], vn_ref[...], ck, cq,
                      jnp.full((tq, 1), NEG_BIG, F32), jnp.zeros((tq, 1), F32),
                      jnp.zeros((tq, HEAD_DIM), F32), cols <= rows)
    m, l, acc = _fox_tile(q, kc_ref[...].astype(BF16), vc_ref[...].astype(BF16), c_ref[:, 0:past], cq,
                          *carry, None)
    o_ref[...] = _head_norm_gate(acc / l, g_ref[...], gate_ref[...])


def _sample_attention(kernel, q, k_new, v_new, k_cache, v_cache, gate, gain, cum=None):
    s, tq, d_grp = q.shape
    n_heads = d_grp // HEAD_DIM
    tn, past = k_new.shape[1], k_cache.shape[1]
    head = lambda rows: pl.BlockSpec((None, rows, HEAD_DIM), lambda si, h: (si, 0, h))
    in_specs = [head(tq), head(tn), head(tn), head(past), head(past)]
    args = [q, k_new, v_new, k_cache, v_cache]
    if cum is not None:
        in_specs.append(pl.BlockSpec((None, 1, cum.shape[2]), lambda si, h: (si * n_heads + h, 0, 0)))
        args.append(cum)
    in_specs += [head(tq), pl.BlockSpec((1, HEAD_DIM), lambda si, h: (0, h))]
    args += [gate, gain]
    return pl.pallas_call(
        kernel,
        grid=(s, n_heads),
        in_specs=in_specs,
        out_specs=head(tq),
        out_shape=jax.ShapeDtypeStruct((s, tq, d_grp), BF16),
        compiler_params=_cparams(2),
        name=kernel.__name__.strip("_"),
    )(*args)


def _out_proj_kernel(final, x_ref, osb_ref, ofx_ref, p_ref, wo_ref, wp_ref, gp_ref, wg_ref, gf_ref, y_ref):
    d_sb = osb_ref.shape[1]
    x1 = x_ref[...] + _dot(osb_ref[...], wo_ref[0:d_sb, :]) + _dot(ofx_ref[...], wo_ref[d_sb:, :])
    r = _rms_rows(x1, gp_ref[...]).astype(BF16)
    gate = 1.0 / (1.0 + jnp.exp(-_dot(r, wg_ref[...])))
    x2 = x1 + _dot(p_ref[...].astype(BF16), wp_ref[...]) * gate
    y_ref[...] = _rms_rows(x2, gf_ref[...]) if final else x2


def _out_proj(x, o_sb, o_fx, p, w_out, w_ple, g_ple, w_gate, g_final, final):
    n, d_model = x.shape
    tm = min(ROW_TILE, n)
    assert n % tm == 0
    row = lambda i: (i, 0)
    fixed = lambda i: (0, 0)
    rows = lambda a: pl.BlockSpec((tm, a.shape[1]), row)
    whole = lambda a: _resident(a.shape, fixed)
    return pl.pallas_call(
        functools.partial(_out_proj_kernel, final),
        grid=(n // tm,),
        in_specs=[rows(x), rows(o_sb), rows(o_fx), rows(p), whole(w_out), whole(w_ple), whole(g_ple),
                  whole(w_gate), whole(g_final)],
        out_specs=pl.BlockSpec((tm, d_model), row),
        out_shape=jax.ShapeDtypeStruct((n, d_model), F32),
        compiler_params=_cparams(1),
        name="out_proj",
    )(x, o_sb, o_fx, p, w_out, w_ple, g_ple, w_gate, g_final)


def _pad_rows(a, rows):
    return jnp.pad(a, ((0, 0), (0, rows - a.shape[1]), (0, 0)))


def kernel(x_prompt, x_sample, p_prompt, p_sample, cache_sb_k, cache_sb_v, cache_fox_k, cache_fox_v, cache_fox_logf, w_in, b_forget, g_attn_norm, g_out_sb, g_out_fox, w_out, w_ple, g_ple_norm, w_ple_gate, g_final):
    depth = w_in.shape[0]
    b, t, d_model = x_prompt.shape
    s, ts, _ = x_sample.shape
    past = cache_sb_k.shape[2]
    n_fox = b_forget.shape[1]
    n_sb = cache_sb_k.shape[3]
    d_sb, d_fox = n_sb * HEAD_DIM, n_fox * HEAD_DIM
    assert d_sb == d_fox and w_in.shape[2] == 4 * d_sb + 4 * d_fox + n_fox

    w_in_b = jnp.pad(w_in, ((0, 0), (0, 0), (0, LANES_V7X - n_fox))).astype(BF16)
    b_f = jnp.pad(b_forget, ((0, 0), (0, LANES_V7X - n_fox)))[:, None, :]
    w_out_b, w_ple_b, w_gate_b = w_out.astype(BF16), w_ple.astype(BF16), w_ple_gate.astype(BF16)
    g_fin = g_final[None, :]

    xp = x_prompt.reshape(b * t, d_model)
    xs = x_sample.reshape(s * ts, d_model)
    slabs_p = slabs_s = None
    logf_p, logf_s = [], []
    for i in range(depth):
        g_attn, g_ple = g_attn_norm[i][None, :], g_ple_norm[i][None, :]
        g_sb, g_fx = g_out_sb[i][None, :], g_out_fox[i][None, :]
        final = i == depth - 1

        (qs, ks, vs, ksb, vsb, gs, qf, kf, vf, kfb, vfb, gf, lf) = _in_proj(
            xp, g_attn, w_in_b[i], b_f[i], i, depth, slabs_p)
        slabs_p = (ks, vs, kf, vf)
        logf = lf[:, :n_fox].reshape(b, t, n_fox)
        logf_p.append(logf)
        cum = _cumsum_lanes(jnp.swapaxes(logf, 1, 2).reshape(b * n_fox, t)).reshape(b * n_fox, 1, t)
        b3 = lambda a: a.reshape(b, t, a.shape[-1])
        o_sb = _prompt_attention(_sb_prompt_kernel, b3(qs), b3(ksb), b3(vsb), b3(gs), g_sb)
        o_fx = _prompt_attention(_fox_prompt_kernel, b3(qf), b3(kfb), b3(vfb), b3(gf), g_fx, cum)
        xp = _out_proj(xp, o_sb.reshape(b * t, d_sb), o_fx.reshape(b * t, d_fox),
                       p_prompt[i].reshape(b * t, -1), w_out_b[i], w_ple_b[i], g_ple, w_gate_b[i], g_fin, final)

        (qs, ks, vs, ksb, vsb, gs, qf, kf, vf, kfb, vfb, gf, lf) = _in_proj(
            xs, g_attn, w_in_b[i], b_f[i], i, depth, slabs_s)
        slabs_s = (ks, vs, kf, vf)
        logf = lf[:, :n_fox].reshape(s, ts, n_fox)
        logf_s.append(logf)
        s3 = lambda a: a.reshape(s, ts, a.shape[-1])
        new = lambda a: _pad_rows(s3(a), LANES_V7X)
        logf_all = jnp.concatenate([cache_fox_logf[i], logf], axis=1)
        logf_all = jnp.swapaxes(_pad_rows(logf_all, past + LANES_V7X), 1, 2)
        cum = _cumsum_lanes(logf_all.reshape(s * n_fox, past + LANES_V7X)).reshape(s * n_fox, 1, -1)
        cache = lambda a: a[i].reshape(s, past, -1)
        o_sb = _sample_attention(_sb_sample_kernel, s3(qs), new(ksb), new(vsb),
                                 cache(cache_sb_k), cache(cache_sb_v), s3(gs), g_sb)
        o_fx = _sample_attention(_fox_sample_kernel, s3(qf), new(kfb), new(vfb),
                                 cache(cache_fox_k), cache(cache_fox_v), s3(gf), g_fx, cum)
        xs = _out_proj(xs, o_sb.reshape(s * ts, d_sb), o_fx.reshape(s * ts, d_fox),
                       p_sample[i].reshape(s * ts, -1), w_out_b[i], w_ple_b[i], g_ple, w_gate_b[i], g_fin, final)

    heads_p = lambda a, n: a.reshape(depth, b, t, n, HEAD_DIM)
    heads_s = lambda a, n: a.reshape(depth, s, ts, n, HEAD_DIM)
    return (xp.reshape(b, t, d_model), xs.reshape(s, ts, d_model),
            heads_p(slabs_p[0], n_sb), heads_p(slabs_p[1], n_sb),
            heads_p(slabs_p[2], n_fox), heads_p(slabs_p[3], n_fox), jnp.stack(logf_p, axis=0),
            heads_s(slabs_s[0], n_sb), heads_s(slabs_s[1], n_sb),
            heads_s(slabs_s[2], n_fox), heads_s(slabs_s[3], n_fox), jnp.stack(logf_s, axis=0))
```

```python
import functools
import math

import jax
import jax.numpy as jnp
from jax import lax
from jax.experimental import pallas as pl
from jax.experimental.pallas import tpu as pltpu

F32 = jnp.float32
BF16 = jnp.bfloat16

HEAD_DIM = 128
EPS = 1e-6
NEG_BIG = -1e30
LOG2E = math.log2(math.e)
ATTN_SCALE = HEAD_DIM ** -0.5

LANES_V7X = 128
VMEM_LIMIT_BYTES_V7X = 56 * 1024 * 1024

SB_UNDERFLOW_LOG = 110.0

SB_DONE = 1e30

ROW_TILE = 256
SB_TILE = 256
SB_CHAINS = 4
FOX_Q_TILE = 1024
FOX_K_TILE = 512


def _cparams(n_axes):
    return pltpu.CompilerParams(
        dimension_semantics=("arbitrary",) * n_axes,
        vmem_limit_bytes=VMEM_LIMIT_BYTES_V7X,
    )


def _resident(block_shape, index_map):
    return pl.BlockSpec(block_shape, index_map, pipeline_mode=pl.Buffered(1))


def _rms_rows(x, gain):
    var = jnp.mean(x * x, axis=-1, keepdims=True)
    return (x * lax.rsqrt(var + EPS)) * gain


def _silu(z):
    return z * (1.0 / (1.0 + jnp.exp(-z)))


def _log_sigmoid(y):
    return jnp.minimum(y, 0.0) - jnp.log(1.0 + jnp.exp(-jnp.abs(y)))


def _dot_nt(a, b):
    return lax.dot_general(a, b, (((1,), (1,)), ((), ())), preferred_element_type=F32)


def _dot(a, b):
    return jnp.dot(a, b, preferred_element_type=F32)


def _in_proj_kernel(d_grp, x_ref, g_ref, w_ref, bf_ref, *refs):
    outs = refs[-13:]
    (qs_ref, ks_ref, vs_ref, ksb_ref, vsb_ref, gs_ref,
     qf_ref, kf_ref, vf_ref, kfb_ref, vfb_ref, gf_ref, lf_ref) = outs
    hb = _rms_rows(x_ref[...], g_ref[...]).astype(BF16)

    def proj(col, width):
        return _dot(hb, w_ref[:, col * d_grp: col * d_grp + width])

    qs_ref[...] = (proj(0, d_grp) * ATTN_SCALE).astype(BF16)
    k = proj(1, d_grp)
    ks_ref[...] = k
    ksb_ref[...] = k.astype(BF16)
    v = proj(2, d_grp)
    vs_ref[...] = v
    vsb_ref[...] = v.astype(BF16)
    gs_ref[...] = _silu(proj(3, d_grp))
    qf_ref[...] = (proj(4, d_grp) * (ATTN_SCALE * LOG2E)).astype(BF16)
    k = proj(5, d_grp)
    kf_ref[...] = k
    kfb_ref[...] = k.astype(BF16)
    v = proj(6, d_grp)
    vf_ref[...] = v
    vfb_ref[...] = v.astype(BF16)
    gf_ref[...] = _silu(proj(7, d_grp))
    lf_ref[...] = _log_sigmoid(proj(8, LANES_V7X) + bf_ref[...])


def _in_proj(x, g, w, bf, layer, depth, slabs):
    n, d_model = x.shape
    d_grp = (w.shape[1] - LANES_V7X) // 8
    tm = min(ROW_TILE, n)
    assert n % tm == 0
    row = lambda i: (i, 0)
    fixed = lambda i: (0, 0)
    slab_spec = pl.BlockSpec((None, tm, d_grp), lambda i: (layer, i, 0))
    grp_f32 = jax.ShapeDtypeStruct((n, d_grp), F32)
    grp_bf16 = jax.ShapeDtypeStruct((n, d_grp), BF16)
    slab = jax.ShapeDtypeStruct((depth, n, d_grp), F32)
    grp_spec = pl.BlockSpec((tm, d_grp), row)
    out_shape = (grp_bf16, slab, slab, grp_bf16, grp_bf16, grp_f32,
                 grp_bf16, slab, slab, grp_bf16, grp_bf16, grp_f32,
                 jax.ShapeDtypeStruct((n, LANES_V7X), F32))
    out_specs = (grp_spec, slab_spec, slab_spec, grp_spec, grp_spec, grp_spec,
                 grp_spec, slab_spec, slab_spec, grp_spec, grp_spec, grp_spec,
                 pl.BlockSpec((tm, LANES_V7X), row))
    in_specs = [pl.BlockSpec((tm, d_model), row), _resident((1, d_model), fixed),
                _resident(w.shape, fixed), _resident((1, LANES_V7X), fixed)]
    args = [x, g, w, bf]
    aliases = {}
    if slabs is not None:
        in_specs += [pl.BlockSpec(memory_space=pl.ANY)] * 4
        args += list(slabs)
        aliases = {4: 1, 5: 2, 6: 7, 7: 8}
    return pl.pallas_call(
        functools.partial(_in_proj_kernel, d_grp),
        grid=(n // tm,),
        in_specs=in_specs,
        out_specs=out_specs,
        out_shape=out_shape,
        input_output_aliases=aliases,
        compiler_params=_cparams(1),
        name="in_proj",
    )(*args)


def _cumsum_kernel(x_ref, o_ref):
    x = x_ref[...]
    length = x.shape[1]
    lane = lax.broadcasted_iota(jnp.int32, x.shape, 1)
    shift = 1
    while shift < length:
        x = x + jnp.where(lane >= shift, pltpu.roll(x, shift, axis=1), 0.0)
        shift *= 2
    o_ref[...] = x * LOG2E


def _cumsum_lanes(x):
    return pl.pallas_call(
        _cumsum_kernel,
        out_shape=jax.ShapeDtypeStruct(x.shape, F32),
        compiler_params=pltpu.CompilerParams(vmem_limit_bytes=VMEM_LIMIT_BYTES_V7X),
        name="cumsum",
    )(x)


def _suffix_ones(tk):
    p = lax.broadcasted_iota(jnp.int32, (tk, tk), 0)
    c = lax.broadcasted_iota(jnp.int32, (tk, tk), 1)
    return jnp.where(p >= c, 1.0, 0.0).astype(BF16)


def _sb_tiles(qs, kts, vts, runs, tri, mask):
    zs = [_dot_nt(q, kt) for q, kt in zip(qs, kts)]
    sps = [jnp.maximum(z, 0.0) + jnp.log(1.0 + jnp.exp(-jnp.abs(z))) for z in zs]
    if mask is not None:
        sps = [jnp.where(mask, sp, 0.0) for sp in sps]
    his = [sp.astype(BF16) for sp in sps]
    los = [(sp - hi.astype(F32)).astype(BF16) for sp, hi in zip(sps, his)]
    cums = [_dot(hi, tri) + _dot(lo, tri) for hi, lo in zip(his, los)]
    ws = [jnp.exp(z - cum - run) for z, cum, run in zip(zs, cums, runs)]
    if mask is not None:
        ws = [jnp.where(mask, w, 0.0) for w in ws]
    pvs = [_dot(w.astype(BF16), vt) for w, vt in zip(ws, vts)]
    return pvs, [run + cum[:, 0:1] for run, cum in zip(runs, cums)]


def _fox_tile(q, kt, vt, ck, cq, m, l, acc, mask):
    u = _dot_nt(q, kt) - ck
    if mask is not None:
        u = jnp.where(mask, u, NEG_BIG)
    m_new = jnp.maximum(m, jnp.max(u, axis=1, keepdims=True) + cq)
    p = jnp.exp2(u - (m_new - cq))
    alpha = jnp.exp2(m - m_new)
    l = alpha * l + jnp.sum(p, axis=1, keepdims=True)
    acc = alpha * acc + _dot(p.astype(BF16), vt)
    return m_new, l, acc


def _head_norm_gate(o, gain, gate):
    ms = jnp.mean(o * o, axis=-1, keepdims=True)
    return ((o * lax.rsqrt(ms + EPS)) * gain * gate).astype(BF16)


def _tile_iotas(tq, tk):
    return (lax.broadcasted_iota(jnp.int32, (tq, tk), 0), lax.broadcasted_iota(jnp.int32, (tq, tk), 1))


def _sb_prompt_kernel(q_ref, k_ref, v_ref, gate_ref, g_ref, o_ref):
    t = k_ref.shape[0]
    tq = min(SB_TILE, t)
    n_ch = q_ref.shape[0] // tq
    first = pl.program_id(2) * n_ch
    tri = _suffix_ones(tq)
    rows, cols = _tile_iotas(tq, tq)
    qs = [q_ref[c * tq:(c + 1) * tq, :] for c in range(n_ch)]

    def kv_tiles(js):
        starts = [pl.multiple_of(j * tq, tq) for j in js]
        return [k_ref[pl.ds(s, tq), :] for s in starts], [v_ref[pl.ds(s, tq), :] for s in starts]

    accs, runs = _sb_tiles(qs, *kv_tiles([first + c for c in range(n_ch)]),
                           [jnp.zeros((tq, 1), F32)] * n_ch, tri, cols < rows)

    def more(carry):
        it, _, _, mins = carry
        live = [jnp.logical_and(first + c - it >= 0, mins[c] < SB_UNDERFLOW_LOG) for c in range(n_ch)]
        return functools.reduce(jnp.logical_or, live)

    def step(carry):
        it, accs, runs, _ = carry
        js = [first + c - it for c in range(n_ch)]
        runs = [jnp.where(j >= 0, run, SB_DONE) for j, run in zip(js, runs)]
        pvs, runs = _sb_tiles(qs, *kv_tiles([jnp.maximum(j, 0) for j in js]), runs, tri, None)
        accs = [acc + pv for acc, pv in zip(accs, pvs)]
        return it + 1, tuple(accs), tuple(runs), tuple(jnp.min(run) for run in runs)

    mins = tuple(jnp.min(run) for run in runs)
    _, accs, _, _ = lax.while_loop(more, step, (jnp.int32(1), tuple(accs), tuple(runs), mins))
    for c in range(n_ch):
        sl = slice(c * tq, (c + 1) * tq)
        o_ref[sl, :] = _head_norm_gate(accs[c], g_ref[...], gate_ref[sl, :])


def _row_to_column(row):
    n = row.shape[1]
    w = min(LANES_V7X, n)
    rows, cols = _tile_iotas(w, w)
    parts = [jnp.sum(jnp.where(rows == cols, row[:, r:r + w], 0.0), axis=1, keepdims=True)
             for r in range(0, n, w)]
    return jnp.concatenate(parts, axis=0)


def _fox_prompt_kernel(q_ref, k_ref, v_ref, c_ref, gate_ref, g_ref, o_ref, sa_ref, sb_ref, mxa_ref, mxb_ref):
    tq = q_ref.shape[0]
    tk = sa_ref.shape[1]
    n_diag = tq // tk
    n_rep = tk // LANES_V7X
    i = pl.program_id(2)
    q = q_ref[...]
    rows, cols = _tile_iotas(tq, tk)
    cq = jnp.broadcast_to(_row_to_column(c_ref[:, pl.ds(pl.multiple_of(i * tq, tq), tq)]), (tq, LANES_V7X))
    ones = jnp.ones((tk, HEAD_DIM), BF16)
    bufs = ((sa_ref, mxa_ref), (sb_ref, mxb_ref))

    def scores(j, buf, mask):
        s_ref, mx_ref = buf
        s = pl.multiple_of(j * tk, tk)
        u = _dot_nt(q, k_ref[pl.ds(s, tk), :]) - c_ref[:, pl.ds(s, tk)]
        if mask is not None:
            u = jnp.where(mask, u, NEG_BIG)
        s_ref[...] = u
        mx_ref[...] = functools.reduce(
            jnp.maximum, [u[:, g * LANES_V7X:(g + 1) * LANES_V7X] for g in range(n_rep)])

    def absorb(j, buf, m, accp):
        s_ref, mx_ref = buf
        s = pl.multiple_of(j * tk, tk)
        m_new = jnp.maximum(m, jnp.max(mx_ref[...], axis=1, keepdims=True) + cq)
        shift = m_new - cq
        alpha = jnp.exp2(m - m_new)
        p = jnp.exp2(s_ref[...] - jnp.concatenate([shift] * n_rep, axis=1)).astype(BF16)
        v1 = jnp.concatenate([v_ref[pl.ds(s, tk), :], ones], axis=1)
        return m_new, jnp.concatenate([alpha, alpha], axis=1) * accp + _dot(p, v1)

    m = jnp.full((tq, LANES_V7X), NEG_BIG, F32)
    accp = jnp.zeros((tq, 2 * HEAD_DIM), F32)
    n_past = i * n_diag
    scores(n_past, bufs[0], cols <= rows)
    for d in range(n_diag):
        if d + 1 < n_diag:
            scores(n_past + d + 1, bufs[(d + 1) % 2], cols + (d + 1) * tk <= rows)
        else:
            scores(jnp.maximum(n_past - 1, 0), bufs[(d + 1) % 2], None)
        m, accp = absorb(n_past + d, bufs[d % 2], m, accp)

    def pair(t, carry):
        cur = n_past - 1 - 2 * t
        scores(cur - 1, bufs[1], None)
        carry = absorb(cur, bufs[0], *carry)
        scores(jnp.maximum(cur - 2, 0), bufs[0], None)
        return absorb(cur - 1, bufs[1], *carry)

    m, accp = lax.fori_loop(0, n_past // 2, pair, (m, accp))
    o_ref[...] = _head_norm_gate(accp[:, :HEAD_DIM] / accp[:, HEAD_DIM:], g_ref[...], gate_ref[...])


def _prompt_attention(kernel, q_rows, q, k, v, gate, gain, cum=None, k_rows=None):
    b, t, d_grp = q.shape
    n_heads = d_grp // HEAD_DIM
    tq = min(q_rows, t)
    assert t % tq == 0
    scratch = []
    if k_rows is not None:
        tk = min(k_rows, tq)
        assert tq % (2 * tk) == 0
        scratch = [pltpu.VMEM((tq, tk), F32)] * 2 + [pltpu.VMEM((tq, LANES_V7X), F32)] * 2
    q_spec = pl.BlockSpec((None, tq, HEAD_DIM), lambda bi, h, i: (bi, i, h))
    kv_spec = pl.BlockSpec((None, t, HEAD_DIM), lambda bi, h, i: (bi, 0, h))
    in_specs = [q_spec, kv_spec, kv_spec]
    args = [q, k, v]
    if cum is not None:
        in_specs.append(pl.BlockSpec((None, 1, t), lambda bi, h, i: (bi * n_heads + h, 0, 0)))
        args.append(cum)
    in_specs += [q_spec, pl.BlockSpec((1, HEAD_DIM), lambda bi, h, i: (0, h))]
    args += [gate, gain]
    return pl.pallas_call(
        kernel,
        grid=(b, n_heads, t // tq),
        in_specs=in_specs,
        out_specs=q_spec,
        out_shape=jax.ShapeDtypeStruct((b, t, d_grp), BF16),
        scratch_shapes=scratch,
        compiler_params=_cparams(3),
        name=kernel.__name__.strip("_"),
    )(*args)


def _sb_sample_kernel(q_ref, kn_ref, vn_ref, kc_ref, vc_ref, gate_ref, g_ref, o_ref):
    tq, tn = q_ref.shape[0], kn_ref.shape[0]
    past = kc_ref.shape[0]
    tk = min(SB_TILE, past)
    q = q_ref[...]
    rows, cols = _tile_iotas(tq, tn)
    (acc,), run = _sb_tiles([q], [kn_ref[...]], [vn_ref[...]], [jnp.zeros((tq, 1), F32)],
                            _suffix_ones(tn), cols < rows)
    tri = _suffix_ones(tk)
    for s in range(past - tk, -1, -tk):
        (pv,), run = _sb_tiles([q], [kc_ref[s:s + tk, :].astype(BF16)], [vc_ref[s:s + tk, :].astype(BF16)],
                               run, tri, None)
        acc = acc + pv
    o_ref[...] = _head_norm_gate(acc, g_ref[...], gate_ref[...])


def _fox_sample_kernel(q_ref, kn_ref, vn_ref, kc_ref, vc_ref, c_ref, gate_ref, g_ref, o_ref):
    tq, tn = q_ref.shape[0], kn_ref.shape[0]
    past = kc_ref.shape[0]
    q = q_ref[...]
    rows, cols = _tile_iotas(tq, tn)
    ck = c_ref[:, past:past + tn]
    cq = jnp.sum(jnp.where(rows == cols, ck, 0.0), axis=1, keepdims=True)
    carry = _fox_tile(q, kn_ref[...], vn_ref[...], ck, cq,
                      jnp.full((tq, 1), NEG_BIG, F32), jnp.zeros((tq, 1), F32),
                      jnp.zeros((tq, HEAD_DIM), F32), cols <= rows)
    m, l, acc = _fox_tile(q, kc_ref[...].astype(BF16), vc_ref[...].astype(BF16), c_ref[:, 0:past], cq,
                          *carry, None)
    o_ref[...] = _head_norm_gate(acc / l, g_ref[...], gate_ref[...])


def _sample_attention(kernel, q, k_new, v_new, k_cache, v_cache, gate, gain, cum=None):
    s, tq, d_grp = q.shape
    n_heads = d_grp // HEAD_DIM
    tn, past = k_new.shape[1], k_cache.shape[1]
    head = lambda rows: pl.BlockSpec((None, rows, HEAD_DIM), lambda si, h: (si, 0, h))
    in_specs = [head(tq), head(tn), head(tn), head(past), head(past)]
    args = [q, k_new, v_new, k_cache, v_cache]
    if cum is not None:
        in_specs.append(pl.BlockSpec((None, 1, cum.shape[2]), lambda si, h: (si * n_heads + h, 0, 0)))
        args.append(cum)
    in_specs += [head(tq), pl.BlockSpec((1, HEAD_DIM), lambda si, h: (0, h))]
    args += [gate, gain]
    return pl.pallas_call(
        kernel,
        grid=(s, n_heads),
        in_specs=in_specs,
        out_specs=head(tq),
        out_shape=jax.ShapeDtypeStruct((s, tq, d_grp), BF16),
        compiler_params=_cparams(2),
        name=kernel.__name__.strip("_"),
    )(*args)


def _out_proj_kernel(final, x_ref, osb_ref, ofx_ref, p_ref, wo_ref, wp_ref, gp_ref, wg_ref, gf_ref, y_ref):
    d_sb = osb_ref.shape[1]
    x1 = x_ref[...] + _dot(osb_ref[...], wo_ref[0:d_sb, :]) + _dot(ofx_ref[...], wo_ref[d_sb:, :])
    r = _rms_rows(x1, gp_ref[...]).astype(BF16)
    gate = 1.0 / (1.0 + jnp.exp(-_dot(r, wg_ref[...])))
    x2 = x1 + _dot(p_ref[...].astype(BF16), wp_ref[...]) * gate
    y_ref[...] = _rms_rows(x2, gf_ref[...]) if final else x2


def _out_proj(x, o_sb, o_fx, p, w_out, w_ple, g_ple, w_gate, g_final, final):
    n, d_model = x.shape
    tm = min(ROW_TILE, n)
    assert n % tm == 0
    row = lambda i: (i, 0)
    fixed = lambda i: (0, 0)
    rows = lambda a: pl.BlockSpec((tm, a.shape[1]), row)
    whole = lambda a: _resident(a.shape, fixed)
    return pl.pallas_call(
        functools.partial(_out_proj_kernel, final),
        grid=(n // tm,),
        in_specs=[rows(x), rows(o_sb), rows(o_fx), rows(p), whole(w_out), whole(w_ple), whole(g_ple),
                  whole(w_gate), whole(g_final)],
        out_specs=pl.BlockSpec((tm, d_model), row),
        out_shape=jax.ShapeDtypeStruct((n, d_model), F32),
        compiler_params=_cparams(1),
        name="out_proj",
    )(x, o_sb, o_fx, p, w_out, w_ple, g_ple, w_gate, g_final)


def _pad_rows(a, rows):
    return jnp.pad(a, ((0, 0), (0, rows - a.shape[1]), (0, 0)))


def kernel(x_prompt, x_sample, p_prompt, p_sample, cache_sb_k, cache_sb_v, cache_fox_k, cache_fox_v, cache_fox_logf, w_in, b_forget, g_attn_norm, g_out_sb, g_out_fox, w_out, w_ple, g_ple_norm, w_ple_gate, g_final):
    depth = w_in.shape[0]
    b, t, d_model = x_prompt.shape
    s, ts, _ = x_sample.shape
    past = cache_sb_k.shape[2]
    n_fox = b_forget.shape[1]
    n_sb = cache_sb_k.shape[3]
    d_sb, d_fox = n_sb * HEAD_DIM, n_fox * HEAD_DIM
    assert d_sb == d_fox and w_in.shape[2] == 4 * d_sb + 4 * d_fox + n_fox

    w_in_b = jnp.pad(w_in, ((0, 0), (0, 0), (0, LANES_V7X - n_fox))).astype(BF16)
    b_f = jnp.pad(b_forget, ((0, 0), (0, LANES_V7X - n_fox)))[:, None, :]
    w_out_b, w_ple_b, w_gate_b = w_out.astype(BF16), w_ple.astype(BF16), w_ple_gate.astype(BF16)
    g_fin = g_final[None, :]

    xp = x_prompt.reshape(b * t, d_model)
    xs = x_sample.reshape(s * ts, d_model)
    slabs_p = slabs_s = None
    logf_p, logf_s = [], []
    for i in range(depth):
        g_attn, g_ple = g_attn_norm[i][None, :], g_ple_norm[i][None, :]
        g_sb, g_fx = g_out_sb[i][None, :], g_out_fox[i][None, :]
        final = i == depth - 1

        (qs, ks, vs, ksb, vsb, gs, qf, kf, vf, kfb, vfb, gf, lf) = _in_proj(
            xp, g_attn, w_in_b[i], b_f[i], i, depth, slabs_p)
        slabs_p = (ks, vs, kf, vf)
        logf = lf[:, :n_fox].reshape(b, t, n_fox)
        logf_p.append(logf)
        cum = _cumsum_lanes(jnp.swapaxes(logf, 1, 2).reshape(b * n_fox, t)).reshape(b * n_fox, 1, t)
        b3 = lambda a: a.reshape(b, t, a.shape[-1])
        o_sb = _prompt_attention(_sb_prompt_kernel, SB_TILE * SB_CHAINS, b3(qs), b3(ksb), b3(vsb), b3(gs), g_sb)
        o_fx = _prompt_attention(_fox_prompt_kernel, FOX_Q_TILE, b3(qf), b3(kfb), b3(vfb), b3(gf), g_fx, cum,
                                 FOX_K_TILE)
        xp = _out_proj(xp, o_sb.reshape(b * t, d_sb), o_fx.reshape(b * t, d_fox),
                       p_prompt[i].reshape(b * t, -1), w_out_b[i], w_ple_b[i], g_ple, w_gate_b[i], g_fin, final)

        (qs, ks, vs, ksb, vsb, gs, qf, kf, vf, kfb, vfb, gf, lf) = _in_proj(
            xs, g_attn, w_in_b[i], b_f[i], i, depth, slabs_s)
        slabs_s = (ks, vs, kf, vf)
        logf = lf[:, :n_fox].reshape(s, ts, n_fox)
        logf_s.append(logf)
        s3 = lambda a: a.reshape(s, ts, a.shape[-1])
        new = lambda a: _pad_rows(s3(a), LANES_V7X)
        logf_all = jnp.concatenate([cache_fox_logf[i], logf], axis=1)
        logf_all = jnp.swapaxes(_pad_rows(logf_all, past + LANES_V7X), 1, 2)
        cum = _cumsum_lanes(logf_all.reshape(s * n_fox, past + LANES_V7X)).reshape(s * n_fox, 1, -1)
        cache = lambda a: a[i].reshape(s, past, -1)
        o_sb = _sample_attention(_sb_sample_kernel, s3(qs), new(ksb), new(vsb),
                                 cache(cache_sb_k), cache(cache_sb_v), s3(gs), g_sb)
        o_fx = _sample_attention(_fox_sample_kernel, s3(qf), new(kfb), new(vfb),
                                 cache(cache_fox_k), cache(cache_fox_v), s3(gf), g_fx, cum)
        xs = _out_proj(xs, o_sb.reshape(s * ts, d_sb), o_fx.reshape(s * ts, d_fox),
                       p_sample[i].reshape(s * ts, -1), w_out_b[i], w_ple_b[i], g_ple, w_gate_b[i], g_fin, final)

    heads_p = lambda a, n: a.reshape(depth, b, t, n, HEAD_DIM)
    heads_s = lambda a, n: a.reshape(depth, s, ts, n, HEAD_DIM)
    return (xp.reshape(b, t, d_model), xs.reshape(s, ts, d_model),
            heads_p(slabs_p[0], n_sb), heads_p(slabs_p[1], n_sb),
            heads_p(slabs_p[2], n_fox), heads_p(slabs_p[3], n_fox), jnp.stack(logf_p, axis=0),
            heads_s(slabs_s[0], n_sb), heads_s(slabs_s[1], n_sb),
            heads_s(slabs_s[2], n_fox), heads_s(slabs_s[3], n_fox), jnp.stack(logf_s, axis=0))
```

```python
import functools
import math

import jax
import jax.numpy as jnp
from jax import lax
from jax.experimental import pallas as pl
from jax.experimental.pallas import tpu as pltpu

F32 = jnp.float32
BF16 = jnp.bfloat16

HEAD_DIM = 128
EPS = 1e-6
NEG_BIG = -1e30
LOG2E = math.log2(math.e)
ATTN_SCALE = HEAD_DIM ** -0.5

LANES_V7X = 128
SUBLANES_V7X = 8
VMEM_LIMIT_BYTES_V7X = 56 * 1024 * 1024

SB_UNDERFLOW_LOG = 110.0

SB_DONE = 1e30

FOX_UNDERFLOW_LOG2 = 160.0
FOX_BOUND_SLACK = 1.01

ROW_TILE = 256
SB_TILE = 256
SB_CHAINS = 4
FOX_Q_TILE = 1024
FOX_K_TILE = 512


def _cparams(n_axes):
    return pltpu.CompilerParams(
        dimension_semantics=("arbitrary",) * n_axes,
        vmem_limit_bytes=VMEM_LIMIT_BYTES_V7X,
    )


def _resident(block_shape, index_map):
    return pl.BlockSpec(block_shape, index_map, pipeline_mode=pl.Buffered(1))


def _rms_rows(x, gain):
    var = jnp.mean(x * x, axis=-1, keepdims=True)
    return (x * lax.rsqrt(var + EPS)) * gain


def _silu(z):
    return z * (1.0 / (1.0 + jnp.exp(-z)))


def _log_sigmoid(y):
    return jnp.minimum(y, 0.0) - jnp.log(1.0 + jnp.exp(-jnp.abs(y)))


def _dot_nt(a, b):
    return lax.dot_general(a, b, (((1,), (1,)), ((), ())), preferred_element_type=F32)


def _dot(a, b):
    return jnp.dot(a, b, preferred_element_type=F32)


def _in_proj_kernel(d_grp, x_ref, g_ref, w_ref, bf_ref, *refs):
    outs = refs[-13:]
    (qs_ref, ks_ref, vs_ref, ksb_ref, vsb_ref, gs_ref,
     qf_ref, kf_ref, vf_ref, kfb_ref, vfb_ref, gf_ref, lf_ref) = outs
    hb = _rms_rows(x_ref[...], g_ref[...]).astype(BF16)

    def proj(col, width):
        return _dot(hb, w_ref[:, col * d_grp: col * d_grp + width])

    qs_ref[...] = (proj(0, d_grp) * ATTN_SCALE).astype(BF16)
    k = proj(1, d_grp)
    ks_ref[...] = k
    ksb_ref[...] = k.astype(BF16)
    v = proj(2, d_grp)
    vs_ref[...] = v
    vsb_ref[...] = v.astype(BF16)
    gs_ref[...] = _silu(proj(3, d_grp))
    qf_ref[...] = (proj(4, d_grp) * (ATTN_SCALE * LOG2E)).astype(BF16)
    k = proj(5, d_grp)
    kf_ref[...] = k
    kfb_ref[...] = k.astype(BF16)
    v = proj(6, d_grp)
    vf_ref[...] = v
    vfb_ref[...] = v.astype(BF16)
    gf_ref[...] = _silu(proj(7, d_grp))
    lf_ref[...] = _log_sigmoid(proj(8, LANES_V7X) + bf_ref[...])


def _in_proj(x, g, w, bf, layer, depth, slabs):
    n, d_model = x.shape
    d_grp = (w.shape[1] - LANES_V7X) // 8
    tm = min(ROW_TILE, n)
    assert n % tm == 0
    row = lambda i: (i, 0)
    fixed = lambda i: (0, 0)
    slab_spec = pl.BlockSpec((None, tm, d_grp), lambda i: (layer, i, 0))
    grp_f32 = jax.ShapeDtypeStruct((n, d_grp), F32)
    grp_bf16 = jax.ShapeDtypeStruct((n, d_grp), BF16)
    slab = jax.ShapeDtypeStruct((depth, n, d_grp), F32)
    grp_spec = pl.BlockSpec((tm, d_grp), row)
    out_shape = (grp_bf16, slab, slab, grp_bf16, grp_bf16, grp_f32,
                 grp_bf16, slab, slab, grp_bf16, grp_bf16, grp_f32,
                 jax.ShapeDtypeStruct((n, LANES_V7X), F32))
    out_specs = (grp_spec, slab_spec, slab_spec, grp_spec, grp_spec, grp_spec,
                 grp_spec, slab_spec, slab_spec, grp_spec, grp_spec, grp_spec,
                 pl.BlockSpec((tm, LANES_V7X), row))
    in_specs = [pl.BlockSpec((tm, d_model), row), _resident((1, d_model), fixed),
                _resident(w.shape, fixed), _resident((1, LANES_V7X), fixed)]
    args = [x, g, w, bf]
    aliases = {}
    if slabs is not None:
        in_specs += [pl.BlockSpec(memory_space=pl.ANY)] * 4
        args += list(slabs)
        aliases = {4: 1, 5: 2, 6: 7, 7: 8}
    return pl.pallas_call(
        functools.partial(_in_proj_kernel, d_grp),
        grid=(n // tm,),
        in_specs=in_specs,
        out_specs=out_specs,
        out_shape=out_shape,
        input_output_aliases=aliases,
        compiler_params=_cparams(1),
        name="in_proj",
    )(*args)


def _cumsum_kernel(x_ref, o_ref):
    x = x_ref[...]
    length = x.shape[1]
    lane = lax.broadcasted_iota(jnp.int32, x.shape, 1)
    shift = 1
    while shift < length:
        x = x + jnp.where(lane >= shift, pltpu.roll(x, shift, axis=1), 0.0)
        shift *= 2
    o_ref[...] = x * LOG2E


def _cumsum_lanes(x):
    return pl.pallas_call(
        _cumsum_kernel,
        out_shape=jax.ShapeDtypeStruct(x.shape, F32),
        compiler_params=pltpu.CompilerParams(vmem_limit_bytes=VMEM_LIMIT_BYTES_V7X),
        name="cumsum",
    )(x)


def _suffix_ones(tk):
    p = lax.broadcasted_iota(jnp.int32, (tk, tk), 0)
    c = lax.broadcasted_iota(jnp.int32, (tk, tk), 1)
    return jnp.where(p >= c, 1.0, 0.0).astype(BF16)


def _sb_tiles(qs, kts, vts, runs, tri, mask):
    zs = [_dot_nt(q, kt) for q, kt in zip(qs, kts)]
    sps = [jnp.maximum(z, 0.0) + jnp.log(1.0 + jnp.exp(-jnp.abs(z))) for z in zs]
    if mask is not None:
        sps = [jnp.where(mask, sp, 0.0) for sp in sps]
    his = [sp.astype(BF16) for sp in sps]
    los = [(sp - hi.astype(F32)).astype(BF16) for sp, hi in zip(sps, his)]
    cums = [_dot(hi, tri) + _dot(lo, tri) for hi, lo in zip(his, los)]
    ws = [jnp.exp(z - cum - run) for z, cum, run in zip(zs, cums, runs)]
    if mask is not None:
        ws = [jnp.where(mask, w, 0.0) for w in ws]
    pvs = [_dot(w.astype(BF16), vt) for w, vt in zip(ws, vts)]
    return pvs, [run + cum[:, 0:1] for run, cum in zip(runs, cums)]


def _fox_tile(q, kt, vt, ck, cq, m, l, acc, mask):
    u = _dot_nt(q, kt) - ck
    if mask is not None:
        u = jnp.where(mask, u, NEG_BIG)
    m_new = jnp.maximum(m, jnp.max(u, axis=1, keepdims=True) + cq)
    p = jnp.exp2(u - (m_new - cq))
    alpha = jnp.exp2(m - m_new)
    l = alpha * l + jnp.sum(p, axis=1, keepdims=True)
    acc = alpha * acc + _dot(p.astype(BF16), vt)
    return m_new, l, acc


def _head_norm_gate(o, gain, gate):
    ms = jnp.mean(o * o, axis=-1, keepdims=True)
    return ((o * lax.rsqrt(ms + EPS)) * gain * gate).astype(BF16)


def _tile_iotas(tq, tk):
    return (lax.broadcasted_iota(jnp.int32, (tq, tk), 0), lax.broadcasted_iota(jnp.int32, (tq, tk), 1))


def _sb_prompt_kernel(q_ref, k_ref, v_ref, gate_ref, g_ref, o_ref):
    t = k_ref.shape[0]
    tq = min(SB_TILE, t)
    n_ch = q_ref.shape[0] // tq
    first = pl.program_id(2) * n_ch
    tri = _suffix_ones(tq)
    rows, cols = _tile_iotas(tq, tq)
    qs = [q_ref[c * tq:(c + 1) * tq, :] for c in range(n_ch)]

    def kv_tiles(js):
        starts = [pl.multiple_of(j * tq, tq) for j in js]
        return [k_ref[pl.ds(s, tq), :] for s in starts], [v_ref[pl.ds(s, tq), :] for s in starts]

    accs, runs = _sb_tiles(qs, *kv_tiles([first + c for c in range(n_ch)]),
                           [jnp.zeros((tq, 1), F32)] * n_ch, tri, cols < rows)

    def more(carry):
        it, _, _, mins = carry
        live = [jnp.logical_and(first + c - it >= 0, mins[c] < SB_UNDERFLOW_LOG) for c in range(n_ch)]
        return functools.reduce(jnp.logical_or, live)

    def step(carry):
        it, accs, runs, _ = carry
        js = [first + c - it for c in range(n_ch)]
        runs = [jnp.where(j >= 0, run, SB_DONE) for j, run in zip(js, runs)]
        pvs, runs = _sb_tiles(qs, *kv_tiles([jnp.maximum(j, 0) for j in js]), runs, tri, None)
        accs = [acc + pv for acc, pv in zip(accs, pvs)]
        return it + 1, tuple(accs), tuple(runs), tuple(jnp.min(run) for run in runs)

    mins = tuple(jnp.min(run) for run in runs)
    _, accs, _, _ = lax.while_loop(more, step, (jnp.int32(1), tuple(accs), tuple(runs), mins))
    for c in range(n_ch):
        sl = slice(c * tq, (c + 1) * tq)
        o_ref[sl, :] = _head_norm_gate(accs[c], g_ref[...], gate_ref[sl, :])


def _row_to_column(row):
    n = row.shape[1]
    w = min(LANES_V7X, n)
    rows, cols = _tile_iotas(w, w)
    parts = [jnp.sum(jnp.where(rows == cols, row[:, r:r + w], 0.0), axis=1, keepdims=True)
             for r in range(0, n, w)]
    return jnp.concatenate(parts, axis=0)


def _fox_prompt_kernel(q_ref, k_ref, v_ref, c_ref, gate_ref, g_ref, o_ref,
                       sa_ref, sb_ref, mxa_ref, mxb_ref, kn_ref):
    tq = q_ref.shape[0]
    tk = sa_ref.shape[1]
    n_diag = tq // tk
    n_rep = tk // LANES_V7X
    i = pl.program_id(2)
    q = q_ref[...]
    rows, cols = _tile_iotas(tq, tk)
    cq = jnp.broadcast_to(_row_to_column(c_ref[:, pl.ds(pl.multiple_of(i * tq, tq), tq)]), (tq, LANES_V7X))
    ones = jnp.ones((tk, HEAD_DIM), BF16)
    bufs = ((sa_ref, mxa_ref), (sb_ref, mxb_ref))

    def scores(j, buf, mask):
        s_ref, mx_ref = buf
        s = pl.multiple_of(j * tk, tk)
        u = _dot_nt(q, k_ref[pl.ds(s, tk), :]) - c_ref[:, pl.ds(s, tk)]
        if mask is not None:
            u = jnp.where(mask, u, NEG_BIG)
        s_ref[...] = u
        mx_ref[...] = functools.reduce(
            jnp.maximum, [u[:, g * LANES_V7X:(g + 1) * LANES_V7X] for g in range(n_rep)])

    def absorb(j, buf, m, accp):
        s_ref, mx_ref = buf
        s = pl.multiple_of(j * tk, tk)
        m_new = jnp.maximum(m, jnp.max(mx_ref[...], axis=1, keepdims=True) + cq)
        shift = m_new - cq
        alpha = jnp.exp2(m - m_new)
        p = jnp.exp2(s_ref[...] - jnp.concatenate([shift] * n_rep, axis=1)).astype(BF16)
        v1 = jnp.concatenate([v_ref[pl.ds(s, tk), :], ones], axis=1)
        return m_new, jnp.concatenate([alpha, alpha], axis=1) * accp + _dot(p, v1)

    @pl.when(i == 0)
    def _():
        def chunk(r, best):
            kk = k_ref[pl.ds(pl.multiple_of(r * tq, tq), tq), :].astype(F32)
            return jnp.maximum(best, jnp.sum(kk * kk, axis=1, keepdims=True))
        best = lax.fori_loop(0, k_ref.shape[0] // tq, chunk, jnp.zeros((tq, 1), F32))
        kn_ref[...] = jnp.broadcast_to(jnp.sqrt(jnp.max(best, axis=0, keepdims=True)), kn_ref.shape)

    qf = q.astype(F32)
    reach = jnp.sqrt(jnp.sum(qf * qf, axis=1, keepdims=True)) * (FOX_BOUND_SLACK * kn_ref[0:1, :]) + cq

    def live(j, m):
        c_end = jnp.min(c_ref[:, pl.ds(pl.multiple_of(j * tk, tk), tk)], axis=1, keepdims=True)
        return jnp.max(reach - m - c_end) > -FOX_UNDERFLOW_LOG2

    m = jnp.full((tq, LANES_V7X), NEG_BIG, F32)
    accp = jnp.zeros((tq, 2 * HEAD_DIM), F32)
    n_past = i * n_diag
    scores(n_past, bufs[0], cols <= rows)
    for d in range(n_diag):
        if d + 1 < n_diag:
            scores(n_past + d + 1, bufs[(d + 1) % 2], cols + (d + 1) * tk <= rows)
        else:
            scores(jnp.maximum(n_past - 1, 0), bufs[(d + 1) % 2], None)
        m, accp = absorb(n_past + d, bufs[d % 2], m, accp)

    def more(carry):
        t, _, _, go = carry
        return jnp.logical_and(t < n_past // 2, go)

    def pair(carry):
        t, m, accp, _ = carry
        cur = n_past - 1 - 2 * t
        scores(cur - 1, bufs[1], None)
        m, accp = absorb(cur, bufs[0], m, accp)
        go = live(jnp.maximum(cur - 2, 0), m)
        scores(jnp.maximum(cur - 2, 0), bufs[0], None)
        m, accp = absorb(cur - 1, bufs[1], m, accp)
        return t + 1, m, accp, go

    _, m, accp, _ = lax.while_loop(more, pair, (jnp.int32(0), m, accp, live(jnp.maximum(n_past - 1, 0), m)))
    o_ref[...] = _head_norm_gate(accp[:, :HEAD_DIM] / accp[:, HEAD_DIM:], g_ref[...], gate_ref[...])


def _prompt_attention(kernel, q_rows, q, k, v, gate, gain, cum=None, k_rows=None):
    b, t, d_grp = q.shape
    n_heads = d_grp // HEAD_DIM
    tq = min(q_rows, t)
    assert t % tq == 0
    scratch = []
    if k_rows is not None:
        tk = min(k_rows, tq)
        assert tq % (2 * tk) == 0
        scratch = ([pltpu.VMEM((tq, tk), F32)] * 2 + [pltpu.VMEM((tq, LANES_V7X), F32)] * 2
                   + [pltpu.VMEM((SUBLANES_V7X, LANES_V7X), F32)])
    q_spec = pl.BlockSpec((None, tq, HEAD_DIM), lambda bi, h, i: (bi, i, h))
    kv_spec = pl.BlockSpec((None, t, HEAD_DIM), lambda bi, h, i: (bi, 0, h))
    in_specs = [q_spec, kv_spec, kv_spec]
    args = [q, k, v]
    if cum is not None:
        in_specs.append(pl.BlockSpec((None, 1, t), lambda bi, h, i: (bi * n_heads + h, 0, 0)))
        args.append(cum)
    in_specs += [q_spec, pl.BlockSpec((1, HEAD_DIM), lambda bi, h, i: (0, h))]
    args += [gate, gain]
    return pl.pallas_call(
        kernel,
        grid=(b, n_heads, t // tq),
        in_specs=in_specs,
        out_specs=q_spec,
        out_shape=jax.ShapeDtypeStruct((b, t, d_grp), BF16),
        scratch_shapes=scratch,
        compiler_params=_cparams(3),
        name=kernel.__name__.strip("_"),
    )(*args)


def _sb_sample_kernel(q_ref, kn_ref, vn_ref, kc_ref, vc_ref, gate_ref, g_ref, o_ref):
    tq, tn = q_ref.shape[0], kn_ref.shape[0]
    past = kc_ref.shape[0]
    tk = min(SB_TILE, past)
    q = q_ref[...]
    rows, cols = _tile_iotas(tq, tn)
    (acc,), run = _sb_tiles([q], [kn_ref[...]], [vn_ref[...]], [jnp.zeros((tq, 1), F32)],
                            _suffix_ones(tn), cols < rows)
    tri = _suffix_ones(tk)
    for s in range(past - tk, -1, -tk):
        (pv,), run = _sb_tiles([q], [kc_ref[s:s + tk, :].astype(BF16)], [vc_ref[s:s + tk, :].astype(BF16)],
                               run, tri, None)
        acc = acc + pv
    o_ref[...] = _head_norm_gate(acc, g_ref[...], gate_ref[...])


def _fox_sample_kernel(q_ref, kn_ref, vn_ref, kc_ref, vc_ref, c_ref, gate_ref, g_ref, o_ref):
    tq, tn = q_ref.shape[0], kn_ref.shape[0]
    past = kc_ref.shape[0]
    q = q_ref[...]
    rows, cols = _tile_iotas(tq, tn)
    ck = c_ref[:, past:past + tn]
    cq = jnp.sum(jnp.where(rows == cols, ck, 0.0), axis=1, keepdims=True)
    carry = _fox_tile(q, kn_ref[...], vn_ref[...], ck, cq,
                      jnp.full((tq, 1), NEG_BIG, F32), jnp.zeros((tq, 1), F32),
                      jnp.zeros((tq, HEAD_DIM), F32), cols <= rows)
    m, l, acc = _fox_tile(q, kc_ref[...].astype(BF16), vc_ref[...].astype(BF16), c_ref[:, 0:past], cq,
                          *carry, None)
    o_ref[...] = _head_norm_gate(acc / l, g_ref[...], gate_ref[...])


def _sample_attention(kernel, q, k_new, v_new, k_cache, v_cache, gate, gain, cum=None):
    s, tq, d_grp = q.shape
    n_heads = d_grp // HEAD_DIM
    tn, past = k_new.shape[1], k_cache.shape[1]
    head = lambda rows: pl.BlockSpec((None, rows, HEAD_DIM), lambda si, h: (si, 0, h))
    in_specs = [head(tq), head(tn), head(tn), head(past), head(past)]
    args = [q, k_new, v_new, k_cache, v_cache]
    if cum is not None:
        in_specs.append(pl.BlockSpec((None, 1, cum.shape[2]), lambda si, h: (si * n_heads + h, 0, 0)))
        args.append(cum)
    in_specs += [head(tq), pl.BlockSpec((1, HEAD_DIM), lambda si, h: (0, h))]
    args += [gate, gain]
    return pl.pallas_call(
        kernel,
        grid=(s, n_heads),
        in_specs=in_specs,
        out_specs=head(tq),
        out_shape=jax.ShapeDtypeStruct((s, tq, d_grp), BF16),
        compiler_params=_cparams(2),
        name=kernel.__name__.strip("_"),
    )(*args)


def _out_proj_kernel(final, x_ref, osb_ref, ofx_ref, p_ref, wo_ref, wp_ref, gp_ref, wg_ref, gf_ref, y_ref):
    d_sb = osb_ref.shape[1]
    x1 = x_ref[...] + _dot(osb_ref[...], wo_ref[0:d_sb, :]) + _dot(ofx_ref[...], wo_ref[d_sb:, :])
    r = _rms_rows(x1, gp_ref[...]).astype(BF16)
    gate = 1.0 / (1.0 + jnp.exp(-_dot(r, wg_ref[...])))
    x2 = x1 + _dot(p_ref[...].astype(BF16), wp_ref[...]) * gate
    y_ref[...] = _rms_rows(x2, gf_ref[...]) if final else x2


def _out_proj(x, o_sb, o_fx, p, w_out, w_ple, g_ple, w_gate, g_final, final):
    n, d_model = x.shape
    tm = min(ROW_TILE, n)
    assert n % tm == 0
    row = lambda i: (i, 0)
    fixed = lambda i: (0, 0)
    rows = lambda a: pl.BlockSpec((tm, a.shape[1]), row)
    whole = lambda a: _resident(a.shape, fixed)
    return pl.pallas_call(
        functools.partial(_out_proj_kernel, final),
        grid=(n // tm,),
        in_specs=[rows(x), rows(o_sb), rows(o_fx), rows(p), whole(w_out), whole(w_ple), whole(g_ple),
                  whole(w_gate), whole(g_final)],
        out_specs=pl.BlockSpec((tm, d_model), row),
        out_shape=jax.ShapeDtypeStruct((n, d_model), F32),
        compiler_params=_cparams(1),
        name="out_proj",
    )(x, o_sb, o_fx, p, w_out, w_ple, g_ple, w_gate, g_final)


def _pad_rows(a, rows):
    return jnp.pad(a, ((0, 0), (0, rows - a.shape[1]), (0, 0)))


def kernel(x_prompt, x_sample, p_prompt, p_sample, cache_sb_k, cache_sb_v, cache_fox_k, cache_fox_v, cache_fox_logf, w_in, b_forget, g_attn_norm, g_out_sb, g_out_fox, w_out, w_ple, g_ple_norm, w_ple_gate, g_final):
    depth = w_in.shape[0]
    b, t, d_model = x_prompt.shape
    s, ts, _ = x_sample.shape
    past = cache_sb_k.shape[2]
    n_fox = b_forget.shape[1]
    n_sb = cache_sb_k.shape[3]
    d_sb, d_fox = n_sb * HEAD_DIM, n_fox * HEAD_DIM
    assert d_sb == d_fox and w_in.shape[2] == 4 * d_sb + 4 * d_fox + n_fox

    w_in_b = jnp.pad(w_in, ((0, 0), (0, 0), (0, LANES_V7X - n_fox))).astype(BF16)
    b_f = jnp.pad(b_forget, ((0, 0), (0, LANES_V7X - n_fox)))[:, None, :]
    w_out_b, w_ple_b, w_gate_b = w_out.astype(BF16), w_ple.astype(BF16), w_ple_gate.astype(BF16)
    g_fin = g_final[None, :]

    xp = x_prompt.reshape(b * t, d_model)
    xs = x_sample.reshape(s * ts, d_model)
    slabs_p = slabs_s = None
    logf_p, logf_s = [], []
    for i in range(depth):
        g_attn, g_ple = g_attn_norm[i][None, :], g_ple_norm[i][None, :]
        g_sb, g_fx = g_out_sb[i][None, :], g_out_fox[i][None, :]
        final = i == depth - 1

        (qs, ks, vs, ksb, vsb, gs, qf, kf, vf, kfb, vfb, gf, lf) = _in_proj(
            xp, g_attn, w_in_b[i], b_f[i], i, depth, slabs_p)
        slabs_p = (ks, vs, kf, vf)
        logf = lf[:, :n_fox].reshape(b, t, n_fox)
        logf_p.append(logf)
        cum = _cumsum_lanes(jnp.swapaxes(logf, 1, 2).reshape(b * n_fox, t)).reshape(b * n_fox, 1, t)
        b3 = lambda a: a.reshape(b, t, a.shape[-1])
        o_sb = _prompt_attention(_sb_prompt_kernel, SB_TILE * SB_CHAINS, b3(qs), b3(ksb), b3(vsb), b3(gs), g_sb)
        o_fx = _prompt_attention(_fox_prompt_kernel, FOX_Q_TILE, b3(qf), b3(kfb), b3(vfb), b3(gf), g_fx, cum,
                                 FOX_K_TILE)
        xp = _out_proj(xp, o_sb.reshape(b * t, d_sb), o_fx.reshape(b * t, d_fox),
                       p_prompt[i].reshape(b * t, -1), w_out_b[i], w_ple_b[i], g_ple, w_gate_b[i], g_fin, final)

        (qs, ks, vs, ksb, vsb, gs, qf, kf, vf, kfb, vfb, gf, lf) = _in_proj(
            xs, g_attn, w_in_b[i], b_f[i], i, depth, slabs_s)
        slabs_s = (ks, vs, kf, vf)
        logf = lf[:, :n_fox].reshape(s, ts, n_fox)
        logf_s.append(logf)
        s3 = lambda a: a.reshape(s, ts, a.shape[-1])
        new = lambda a: _pad_rows(s3(a), LANES_V7X)
        logf_all = jnp.concatenate([cache_fox_logf[i], logf], axis=1)
        logf_all = jnp.swapaxes(_pad_rows(logf_all, past + LANES_V7X), 1, 2)
        cum = _cumsum_lanes(logf_all.reshape(s * n_fox, past + LANES_V7X)).reshape(s * n_fox, 1, -1)
        cache = lambda a: a[i].reshape(s, past, -1)
        o_sb = _sample_attention(_sb_sample_kernel, s3(qs), new(ksb), new(vsb),
                                 cache(cache_sb_k), cache(cache_sb_v), s3(gs), g_sb)
        o_fx = _sample_attention(_fox_sample_kernel, s3(qf), new(kfb), new(vfb),
                                 cache(cache_fox_k), cache(cache_fox_v), s3(gf), g_fx, cum)
        xs = _out_proj(xs, o_sb.reshape(s * ts, d_sb), o_fx.reshape(s * ts, d_fox),
                       p_sample[i].reshape(s * ts, -1), w_out_b[i], w_ple_b[i], g_ple, w_gate_b[i], g_fin, final)

    heads_p = lambda a, n: a.reshape(depth, b, t, n, HEAD_DIM)
    heads_s = lambda a, n: a.reshape(depth, s, ts, n, HEAD_DIM)
    return (xp.reshape(b, t, d_model), xs.reshape(s, ts, d_model),
            heads_p(slabs_p[0], n_sb), heads_p(slabs_p[1], n_sb),
            heads_p(slabs_p[2], n_fox), heads_p(slabs_p[3], n_fox), jnp.stack(logf_p, axis=0),
            heads_s(slabs_s[0], n_sb), heads_s(slabs_s[1], n_sb),
            heads_s(slabs_s[2], n_fox), heads_s(slabs_s[3], n_fox), jnp.stack(logf_s, axis=0))
```

```python
import functools
import math

import jax
import jax.numpy as jnp
from jax import lax
from jax.experimental import pallas as pl
from jax.experimental.pallas import tpu as pltpu

F32 = jnp.float32
BF16 = jnp.bfloat16

HEAD_DIM = 128
EPS = 1e-6
NEG_BIG = -1e30
LOG2E = math.log2(math.e)
ATTN_SCALE = HEAD_DIM ** -0.5

LANES_V7X = 128
SUBLANES_V7X = 8
VMEM_LIMIT_BYTES_V7X = 56 * 1024 * 1024

SB_UNDERFLOW_LOG = 110.0

SB_DONE = 1e30

FOX_UNDERFLOW_LOG2 = 160.0
FOX_BOUND_SLACK = 1.01

ROW_TILE = 256
SB_TILE = 256
SB_CHAINS = 4
FOX_Q_TILE = 1024
FOX_K_TILE = 512


def _cparams(n_axes):
    return pltpu.CompilerParams(
        dimension_semantics=("arbitrary",) * n_axes,
        vmem_limit_bytes=VMEM_LIMIT_BYTES_V7X,
    )


def _resident(block_shape, index_map):
    return pl.BlockSpec(block_shape, index_map, pipeline_mode=pl.Buffered(1))


def _rms_rows(x, gain):
    var = jnp.mean(x * x, axis=-1, keepdims=True)
    return (x * lax.rsqrt(var + EPS)) * gain


def _silu(z):
    return z * (1.0 / (1.0 + jnp.exp(-z)))


def _log_sigmoid(y):
    return jnp.minimum(y, 0.0) - jnp.log(1.0 + jnp.exp(-jnp.abs(y)))


def _dot_nt(a, b):
    return lax.dot_general(a, b, (((1,), (1,)), ((), ())), preferred_element_type=F32)


def _dot(a, b):
    return jnp.dot(a, b, preferred_element_type=F32)


def _in_proj_kernel(d_grp, x_ref, g_ref, w_ref, bf_ref, *refs):
    outs = refs[-13:]
    (qs_ref, ks_ref, vs_ref, ksb_ref, vsb_ref, gs_ref,
     qf_ref, kf_ref, vf_ref, kfb_ref, vfb_ref, gf_ref, lf_ref) = outs
    hb = _rms_rows(x_ref[...], g_ref[...]).astype(BF16)

    def proj(col, width):
        return _dot(hb, w_ref[:, col * d_grp: col * d_grp + width])

    def store_heads(ref, a):
        for h in range(d_grp // HEAD_DIM):
            ref[:, h, :] = a[:, h * HEAD_DIM:(h + 1) * HEAD_DIM]

    qs_ref[...] = (proj(0, d_grp) * ATTN_SCALE).astype(BF16)
    k = proj(1, d_grp)
    store_heads(ks_ref, k)
    ksb_ref[...] = k.astype(BF16)
    v = proj(2, d_grp)
    store_heads(vs_ref, v)
    vsb_ref[...] = v.astype(BF16)
    gs_ref[...] = _silu(proj(3, d_grp))
    qf_ref[...] = (proj(4, d_grp) * (ATTN_SCALE * LOG2E)).astype(BF16)
    k = proj(5, d_grp)
    store_heads(kf_ref, k)
    kfb_ref[...] = k.astype(BF16)
    v = proj(6, d_grp)
    store_heads(vf_ref, v)
    vfb_ref[...] = v.astype(BF16)
    gf_ref[...] = _silu(proj(7, d_grp))
    lf_ref[...] = _log_sigmoid(proj(8, LANES_V7X) + bf_ref[...])


def _in_proj(x, g, w, bf, layer, depth, slabs):
    n, d_model = x.shape
    d_grp = (w.shape[1] - LANES_V7X) // 8
    tm = min(ROW_TILE, n)
    assert n % tm == 0
    row = lambda i: (i, 0)
    fixed = lambda i: (0, 0)
    n_heads = d_grp // HEAD_DIM
    slab_spec = pl.BlockSpec((None, tm, n_heads, HEAD_DIM), lambda i: (layer, i, 0, 0))
    grp_f32 = jax.ShapeDtypeStruct((n, d_grp), F32)
    grp_bf16 = jax.ShapeDtypeStruct((n, d_grp), BF16)
    slab = jax.ShapeDtypeStruct((depth, n, n_heads, HEAD_DIM), F32)
    grp_spec = pl.BlockSpec((tm, d_grp), row)
    out_shape = (grp_bf16, slab, slab, grp_bf16, grp_bf16, grp_f32,
                 grp_bf16, slab, slab, grp_bf16, grp_bf16, grp_f32,
                 jax.ShapeDtypeStruct((n, LANES_V7X), F32))
    out_specs = (grp_spec, slab_spec, slab_spec, grp_spec, grp_spec, grp_spec,
                 grp_spec, slab_spec, slab_spec, grp_spec, grp_spec, grp_spec,
                 pl.BlockSpec((tm, LANES_V7X), row))
    in_specs = [pl.BlockSpec((tm, d_model), row), _resident((1, d_model), fixed),
                _resident(w.shape, fixed), _resident((1, LANES_V7X), fixed)]
    args = [x, g, w, bf]
    aliases = {}
    if slabs is not None:
        in_specs += [pl.BlockSpec(memory_space=pl.ANY)] * 4
        args += list(slabs)
        aliases = {4: 1, 5: 2, 6: 7, 7: 8}
    return pl.pallas_call(
        functools.partial(_in_proj_kernel, d_grp),
        grid=(n // tm,),
        in_specs=in_specs,
        out_specs=out_specs,
        out_shape=out_shape,
        input_output_aliases=aliases,
        compiler_params=_cparams(1),
        name="in_proj",
    )(*args)


def _cumsum_kernel(x_ref, o_ref):
    x = x_ref[...]
    length = x.shape[1]
    lane = lax.broadcasted_iota(jnp.int32, x.shape, 1)
    shift = 1
    while shift < length:
        x = x + jnp.where(lane >= shift, pltpu.roll(x, shift, axis=1), 0.0)
        shift *= 2
    o_ref[...] = x * LOG2E


def _cumsum_lanes(x):
    return pl.pallas_call(
        _cumsum_kernel,
        out_shape=jax.ShapeDtypeStruct(x.shape, F32),
        compiler_params=pltpu.CompilerParams(vmem_limit_bytes=VMEM_LIMIT_BYTES_V7X),
        name="cumsum",
    )(x)


def _suffix_ones(tk):
    p = lax.broadcasted_iota(jnp.int32, (tk, tk), 0)
    c = lax.broadcasted_iota(jnp.int32, (tk, tk), 1)
    return jnp.where(p >= c, 1.0, 0.0).astype(BF16)


def _sb_tiles(qs, kts, vts, runs, tri, mask):
    zs = [_dot_nt(q, kt) for q, kt in zip(qs, kts)]
    sps = [jnp.maximum(z, 0.0) + jnp.log(1.0 + jnp.exp(-jnp.abs(z))) for z in zs]
    if mask is not None:
        sps = [jnp.where(mask, sp, 0.0) for sp in sps]
    his = [sp.astype(BF16) for sp in sps]
    los = [(sp - hi.astype(F32)).astype(BF16) for sp, hi in zip(sps, his)]
    cums = [_dot(hi, tri) + _dot(lo, tri) for hi, lo in zip(his, los)]
    ws = [jnp.exp(z - cum - run) for z, cum, run in zip(zs, cums, runs)]
    if mask is not None:
        ws = [jnp.where(mask, w, 0.0) for w in ws]
    pvs = [_dot(w.astype(BF16), vt) for w, vt in zip(ws, vts)]
    return pvs, [run + cum[:, 0:1] for run, cum in zip(runs, cums)]


def _fox_tile(q, kt, vt, ck, cq, m, l, acc, mask):
    u = _dot_nt(q, kt) - ck
    if mask is not None:
        u = jnp.where(mask, u, NEG_BIG)
    m_new = jnp.maximum(m, jnp.max(u, axis=1, keepdims=True) + cq)
    p = jnp.exp2(u - (m_new - cq))
    alpha = jnp.exp2(m - m_new)
    l = alpha * l + jnp.sum(p, axis=1, keepdims=True)
    acc = alpha * acc + _dot(p.astype(BF16), vt)
    return m_new, l, acc


def _head_norm_gate(o, gain, gate):
    ms = jnp.mean(o * o, axis=-1, keepdims=True)
    return ((o * lax.rsqrt(ms + EPS)) * gain * gate).astype(BF16)


def _tile_iotas(tq, tk):
    return (lax.broadcasted_iota(jnp.int32, (tq, tk), 0), lax.broadcasted_iota(jnp.int32, (tq, tk), 1))


def _sb_prompt_kernel(q_ref, k_ref, v_ref, gate_ref, g_ref, o_ref):
    t = k_ref.shape[0]
    tq = min(SB_TILE, t)
    n_ch = q_ref.shape[0] // tq
    first = pl.program_id(2) * n_ch
    tri = _suffix_ones(tq)
    rows, cols = _tile_iotas(tq, tq)
    qs = [q_ref[c * tq:(c + 1) * tq, :] for c in range(n_ch)]

    def kv_tiles(js):
        starts = [pl.multiple_of(j * tq, tq) for j in js]
        return [k_ref[pl.ds(s, tq), :] for s in starts], [v_ref[pl.ds(s, tq), :] for s in starts]

    accs, runs = _sb_tiles(qs, *kv_tiles([first + c for c in range(n_ch)]),
                           [jnp.zeros((tq, 1), F32)] * n_ch, tri, cols < rows)

    def more(carry):
        it, _, _, mins = carry
        live = [jnp.logical_and(first + c - it >= 0, mins[c] < SB_UNDERFLOW_LOG) for c in range(n_ch)]
        return functools.reduce(jnp.logical_or, live)

    def step(carry):
        it, accs, runs, _ = carry
        js = [first + c - it for c in range(n_ch)]
        runs = [jnp.where(j >= 0, run, SB_DONE) for j, run in zip(js, runs)]
        pvs, runs = _sb_tiles(qs, *kv_tiles([jnp.maximum(j, 0) for j in js]), runs, tri, None)
        accs = [acc + pv for acc, pv in zip(accs, pvs)]
        return it + 1, tuple(accs), tuple(runs), tuple(jnp.min(run) for run in runs)

    mins = tuple(jnp.min(run) for run in runs)
    _, accs, _, _ = lax.while_loop(more, step, (jnp.int32(1), tuple(accs), tuple(runs), mins))
    for c in range(n_ch):
        sl = slice(c * tq, (c + 1) * tq)
        o_ref[sl, :] = _head_norm_gate(accs[c], g_ref[...], gate_ref[sl, :])


def _row_to_column(row):
    n = row.shape[1]
    w = min(LANES_V7X, n)
    rows, cols = _tile_iotas(w, w)
    parts = [jnp.sum(jnp.where(rows == cols, row[:, r:r + w], 0.0), axis=1, keepdims=True)
             for r in range(0, n, w)]
    return jnp.concatenate(parts, axis=0)


def _fox_prompt_kernel(q_ref, k_ref, v_ref, c_ref, gate_ref, g_ref, o_ref,
                       sa_ref, sb_ref, mxa_ref, mxb_ref, kn_ref):
    tq = q_ref.shape[0]
    tk = sa_ref.shape[1]
    n_diag = tq // tk
    n_rep = tk // LANES_V7X
    i = pl.program_id(2)
    q = q_ref[...]
    rows, cols = _tile_iotas(tq, tk)
    cq = jnp.broadcast_to(_row_to_column(c_ref[:, pl.ds(pl.multiple_of(i * tq, tq), tq)]), (tq, LANES_V7X))
    ones = jnp.ones((tk, HEAD_DIM), BF16)
    bufs = ((sa_ref, mxa_ref), (sb_ref, mxb_ref))

    def scores(j, buf, mask):
        s_ref, mx_ref = buf
        s = pl.multiple_of(j * tk, tk)
        u = _dot_nt(q, k_ref[pl.ds(s, tk), :]) - c_ref[:, pl.ds(s, tk)]
        if mask is not None:
            u = jnp.where(mask, u, NEG_BIG)
        s_ref[...] = u
        mx_ref[...] = functools.reduce(
            jnp.maximum, [u[:, g * LANES_V7X:(g + 1) * LANES_V7X] for g in range(n_rep)])

    def absorb(j, buf, m, accp):
        s_ref, mx_ref = buf
        s = pl.multiple_of(j * tk, tk)
        m_new = jnp.maximum(m, jnp.max(mx_ref[...], axis=1, keepdims=True) + cq)
        shift = m_new - cq
        alpha = jnp.exp2(m - m_new)
        p = jnp.exp2(s_ref[...] - jnp.concatenate([shift] * n_rep, axis=1)).astype(BF16)
        v1 = jnp.concatenate([v_ref[pl.ds(s, tk), :], ones], axis=1)
        return m_new, jnp.concatenate([alpha, alpha], axis=1) * accp + _dot(p, v1)

    @pl.when(i == 0)
    def _():
        def chunk(r, best):
            kk = k_ref[pl.ds(pl.multiple_of(r * tq, tq), tq), :].astype(F32)
            return jnp.maximum(best, jnp.sum(kk * kk, axis=1, keepdims=True))
        best = lax.fori_loop(0, k_ref.shape[0] // tq, chunk, jnp.zeros((tq, 1), F32))
        kn_ref[...] = jnp.broadcast_to(jnp.sqrt(jnp.max(best, axis=0, keepdims=True)), kn_ref.shape)

    qf = q.astype(F32)
    reach = jnp.sqrt(jnp.sum(qf * qf, axis=1, keepdims=True)) * (FOX_BOUND_SLACK * kn_ref[0:1, :]) + cq

    def live(j, m):
        c_end = jnp.min(c_ref[:, pl.ds(pl.multiple_of(j * tk, tk), tk)], axis=1, keepdims=True)
        return jnp.max(reach - m - c_end) > -FOX_UNDERFLOW_LOG2

    m = jnp.full((tq, LANES_V7X), NEG_BIG, F32)
    accp = jnp.zeros((tq, 2 * HEAD_DIM), F32)
    n_past = i * n_diag
    scores(n_past, bufs[0], cols <= rows)
    for d in range(n_diag):
        if d + 1 < n_diag:
            scores(n_past + d + 1, bufs[(d + 1) % 2], cols + (d + 1) * tk <= rows)
        else:
            scores(jnp.maximum(n_past - 1, 0), bufs[(d + 1) % 2], None)
        m, accp = absorb(n_past + d, bufs[d % 2], m, accp)

    def more(carry):
        t, _, _, go = carry
        return jnp.logical_and(t < n_past // 2, go)

    def pair(carry):
        t, m, accp, _ = carry
        cur = n_past - 1 - 2 * t
        scores(cur - 1, bufs[1], None)
        m, accp = absorb(cur, bufs[0], m, accp)
        go = live(jnp.maximum(cur - 2, 0), m)
        scores(jnp.maximum(cur - 2, 0), bufs[0], None)
        m, accp = absorb(cur - 1, bufs[1], m, accp)
        return t + 1, m, accp, go

    _, m, accp, _ = lax.while_loop(more, pair, (jnp.int32(0), m, accp, live(jnp.maximum(n_past - 1, 0), m)))
    o_ref[...] = _head_norm_gate(accp[:, :HEAD_DIM] / accp[:, HEAD_DIM:], g_ref[...], gate_ref[...])


def _prompt_attention(kernel, q_rows, q, k, v, gate, gain, cum=None, k_rows=None):
    b, t, d_grp = q.shape
    n_heads = d_grp // HEAD_DIM
    tq = min(q_rows, t)
    assert t % tq == 0
    scratch = []
    if k_rows is not None:
        tk = min(k_rows, tq)
        assert tq % (2 * tk) == 0
        scratch = ([pltpu.VMEM((tq, tk), F32)] * 2 + [pltpu.VMEM((tq, LANES_V7X), F32)] * 2
                   + [pltpu.VMEM((SUBLANES_V7X, LANES_V7X), F32)])
    q_spec = pl.BlockSpec((None, tq, HEAD_DIM), lambda bi, h, i: (bi, i, h))
    kv_spec = pl.BlockSpec((None, t, HEAD_DIM), lambda bi, h, i: (bi, 0, h))
    in_specs = [q_spec, kv_spec, kv_spec]
    args = [q, k, v]
    if cum is not None:
        in_specs.append(pl.BlockSpec((None, 1, t), lambda bi, h, i: (bi * n_heads + h, 0, 0)))
        args.append(cum)
    in_specs += [q_spec, pl.BlockSpec((1, HEAD_DIM), lambda bi, h, i: (0, h))]
    args += [gate, gain]
    return pl.pallas_call(
        kernel,
        grid=(b, n_heads, t // tq),
        in_specs=in_specs,
        out_specs=q_spec,
        out_shape=jax.ShapeDtypeStruct((b, t, d_grp), BF16),
        scratch_shapes=scratch,
        compiler_params=_cparams(3),
        name=kernel.__name__.strip("_"),
    )(*args)


def _head_slices(d_grp):
    return [slice(h * HEAD_DIM, (h + 1) * HEAD_DIM) for h in range(d_grp // HEAD_DIM)]


def _sb_sample_kernel(q_ref, kn_ref, vn_ref, kc_ref, vc_ref, gate_ref, g_ref, o_ref):
    tq, tn, past = q_ref.shape[0], kn_ref.shape[0], kc_ref.shape[0]
    tk = min(SB_TILE, past)
    heads = _head_slices(q_ref.shape[1])
    qs = [q_ref[:, sl] for sl in heads]
    rows, cols = _tile_iotas(tq, tn)
    accs, runs = _sb_tiles(qs, [kn_ref[:, sl] for sl in heads], [vn_ref[:, sl] for sl in heads],
                           [jnp.zeros((tq, 1), F32)] * len(heads), _suffix_ones(tn), cols < rows)
    tri = _suffix_ones(tk)
    for s in range(past - tk, -1, -tk):
        pvs, runs = _sb_tiles(qs, [kc_ref[s:s + tk, h, :].astype(BF16) for h in range(len(heads))],
                              [vc_ref[s:s + tk, h, :].astype(BF16) for h in range(len(heads))], runs, tri, None)
        accs = [acc + pv for acc, pv in zip(accs, pvs)]
    for sl, acc in zip(heads, accs):
        o_ref[:, sl] = _head_norm_gate(acc, g_ref[:, sl], gate_ref[:, sl])


def _fox_sample_kernel(q_ref, kn_ref, vn_ref, kc_ref, vc_ref, c_ref, gate_ref, g_ref, o_ref):
    tq, tn, past = q_ref.shape[0], kn_ref.shape[0], kc_ref.shape[0]
    rows, cols = _tile_iotas(tq, tn)
    for h, sl in enumerate(_head_slices(q_ref.shape[1])):
        q = q_ref[:, sl]
        ck = c_ref[h:h + 1, past:past + tn]
        cq = jnp.sum(jnp.where(rows == cols, ck, 0.0), axis=1, keepdims=True)
        carry = _fox_tile(q, kn_ref[:, sl], vn_ref[:, sl], ck, cq,
                          jnp.full((tq, 1), NEG_BIG, F32), jnp.zeros((tq, 1), F32),
                          jnp.zeros((tq, HEAD_DIM), F32), cols <= rows)
        m, l, acc = _fox_tile(q, kc_ref[:, h, :].astype(BF16), vc_ref[:, h, :].astype(BF16),
                              c_ref[h:h + 1, 0:past], cq, *carry, None)
        o_ref[:, sl] = _head_norm_gate(acc / l, g_ref[:, sl], gate_ref[:, sl])


def _sample_attention(kernel, layer, q, k_new, v_new, k_cache, v_cache, gate, gain, cum=None):
    s, tq, d_grp = q.shape
    tn = k_new.shape[1]
    _, _, past, n_heads, _ = k_cache.shape
    rows = lambda r: pl.BlockSpec((None, r, d_grp), lambda si: (si, 0, 0))
    cache = pl.BlockSpec((None, None, past, n_heads, HEAD_DIM), lambda si: (layer, si, 0, 0, 0))
    in_specs = [rows(tq), rows(tn), rows(tn), cache, cache]
    args = [q, k_new, v_new, k_cache, v_cache]
    if cum is not None:
        in_specs.append(pl.BlockSpec((None, n_heads, cum.shape[2]), lambda si: (si, 0, 0)))
        args.append(cum)
    in_specs += [rows(tq), pl.BlockSpec((1, d_grp), lambda si: (0, 0))]
    args += [gate, gain]
    return pl.pallas_call(
        kernel,
        grid=(s,),
        in_specs=in_specs,
        out_specs=rows(tq),
        out_shape=jax.ShapeDtypeStruct((s, tq, d_grp), BF16),
        compiler_params=_cparams(1),
        name=kernel.__name__.strip("_"),
    )(*args)


def _out_proj_kernel(final, x_ref, osb_ref, ofx_ref, p_ref, wo_ref, wp_ref, gp_ref, wg_ref, gf_ref, y_ref):
    d_sb = osb_ref.shape[1]
    x1 = x_ref[...] + _dot(osb_ref[...], wo_ref[0:d_sb, :]) + _dot(ofx_ref[...], wo_ref[d_sb:, :])
    r = _rms_rows(x1, gp_ref[...]).astype(BF16)
    gate = 1.0 / (1.0 + jnp.exp(-_dot(r, wg_ref[...])))
    x2 = x1 + _dot(p_ref[...].astype(BF16), wp_ref[...]) * gate
    y_ref[...] = _rms_rows(x2, gf_ref[...]) if final else x2


def _out_proj(x, o_sb, o_fx, p, w_out, w_ple, g_ple, w_gate, g_final, final):
    n, d_model = x.shape
    tm = min(ROW_TILE, n)
    assert n % tm == 0
    row = lambda i: (i, 0)
    fixed = lambda i: (0, 0)
    rows = lambda a: pl.BlockSpec((tm, a.shape[1]), row)
    whole = lambda a: _resident(a.shape, fixed)
    return pl.pallas_call(
        functools.partial(_out_proj_kernel, final),
        grid=(n // tm,),
        in_specs=[rows(x), rows(o_sb), rows(o_fx), rows(p), whole(w_out), whole(w_ple), whole(g_ple),
                  whole(w_gate), whole(g_final)],
        out_specs=pl.BlockSpec((tm, d_model), row),
        out_shape=jax.ShapeDtypeStruct((n, d_model), F32),
        compiler_params=_cparams(1),
        name="out_proj",
    )(x, o_sb, o_fx, p, w_out, w_ple, g_ple, w_gate, g_final)


def _pad_rows(a, rows):
    return jnp.pad(a, ((0, 0), (0, rows - a.shape[1]), (0, 0)))


def kernel(x_prompt, x_sample, p_prompt, p_sample, cache_sb_k, cache_sb_v, cache_fox_k, cache_fox_v, cache_fox_logf, w_in, b_forget, g_attn_norm, g_out_sb, g_out_fox, w_out, w_ple, g_ple_norm, w_ple_gate, g_final):
    depth = w_in.shape[0]
    b, t, d_model = x_prompt.shape
    s, ts, _ = x_sample.shape
    past = cache_sb_k.shape[2]
    n_fox = b_forget.shape[1]
    n_sb = cache_sb_k.shape[3]
    d_sb, d_fox = n_sb * HEAD_DIM, n_fox * HEAD_DIM
    assert d_sb == d_fox and w_in.shape[2] == 4 * d_sb + 4 * d_fox + n_fox

    w_in_b = jnp.pad(w_in, ((0, 0), (0, 0), (0, LANES_V7X - n_fox))).astype(BF16)
    b_f = jnp.pad(b_forget, ((0, 0), (0, LANES_V7X - n_fox)))[:, None, :]
    w_out_b, w_ple_b, w_gate_b = w_out.astype(BF16), w_ple.astype(BF16), w_ple_gate.astype(BF16)
    g_fin = g_final[None, :]

    xp = x_prompt.reshape(b * t, d_model)
    xs = x_sample.reshape(s * ts, d_model)
    slabs_p = slabs_s = None
    logf_p, logf_s = [], []
    for i in range(depth):
        g_attn, g_ple = g_attn_norm[i][None, :], g_ple_norm[i][None, :]
        g_sb, g_fx = g_out_sb[i][None, :], g_out_fox[i][None, :]
        final = i == depth - 1

        (qs, ks, vs, ksb, vsb, gs, qf, kf, vf, kfb, vfb, gf, lf) = _in_proj(
            xp, g_attn, w_in_b[i], b_f[i], i, depth, slabs_p)
        slabs_p = (ks, vs, kf, vf)
        logf = lf[:, :n_fox].reshape(b, t, n_fox)
        logf_p.append(logf)
        cum = _cumsum_lanes(jnp.swapaxes(logf, 1, 2).reshape(b * n_fox, t)).reshape(b * n_fox, 1, t)
        b3 = lambda a: a.reshape(b, t, a.shape[-1])
        o_sb = _prompt_attention(_sb_prompt_kernel, SB_TILE * SB_CHAINS, b3(qs), b3(ksb), b3(vsb), b3(gs), g_sb)
        o_fx = _prompt_attention(_fox_prompt_kernel, FOX_Q_TILE, b3(qf), b3(kfb), b3(vfb), b3(gf), g_fx, cum,
                                 FOX_K_TILE)
        xp = _out_proj(xp, o_sb.reshape(b * t, d_sb), o_fx.reshape(b * t, d_fox),
                       p_prompt[i].reshape(b * t, -1), w_out_b[i], w_ple_b[i], g_ple, w_gate_b[i], g_fin, final)

        (qs, ks, vs, ksb, vsb, gs, qf, kf, vf, kfb, vfb, gf, lf) = _in_proj(
            xs, g_attn, w_in_b[i], b_f[i], i, depth, slabs_s)
        slabs_s = (ks, vs, kf, vf)
        logf = lf[:, :n_fox].reshape(s, ts, n_fox)
        logf_s.append(logf)
        s3 = lambda a: a.reshape(s, ts, a.shape[-1])
        new = lambda a: _pad_rows(s3(a), LANES_V7X)
        logf_all = jnp.concatenate([cache_fox_logf[i], logf], axis=1)
        logf_all = jnp.swapaxes(_pad_rows(logf_all, past + LANES_V7X), 1, 2)
        cum = _cumsum_lanes(logf_all.reshape(s * n_fox, past + LANES_V7X)).reshape(s, n_fox, -1)
        o_sb = _sample_attention(_sb_sample_kernel, i, s3(qs), new(ksb), new(vsb),
                                 cache_sb_k, cache_sb_v, s3(gs), g_sb)
        o_fx = _sample_attention(_fox_sample_kernel, i, s3(qf), new(kfb), new(vfb),
                                 cache_fox_k, cache_fox_v, s3(gf), g_fx, cum)
        xs = _out_proj(xs, o_sb.reshape(s * ts, d_sb), o_fx.reshape(s * ts, d_fox),
                       p_sample[i].reshape(s * ts, -1), w_out_b[i], w_ple_b[i], g_ple, w_gate_b[i], g_fin, final)

    heads_p = lambda a, n: a.reshape(depth, b, t, n, HEAD_DIM)
    heads_s = lambda a, n: a.reshape(depth, s, ts, n, HEAD_DIM)
    return (xp.reshape(b, t, d_model), xs.reshape(s, ts, d_model),
            heads_p(slabs_p[0], n_sb), heads_p(slabs_p[1], n_sb),
            heads_p(slabs_p[2], n_fox), heads_p(slabs_p[3], n_fox), jnp.stack(logf_p, axis=0),
            heads_s(slabs_s[0], n_sb), heads_s(slabs_s[1], n_sb),
            heads_s(slabs_s[2], n_fox), heads_s(slabs_s[3], n_fox), jnp.stack(logf_s, axis=0))
```

```python
import functools
import math

import jax
import jax.numpy as jnp
from jax import lax
from jax.experimental import pallas as pl
from jax.experimental.pallas import tpu as pltpu

F32 = jnp.float32
BF16 = jnp.bfloat16

HEAD_DIM = 128
EPS = 1e-6
NEG_BIG = -1e30
LOG2E = math.log2(math.e)
ATTN_SCALE = HEAD_DIM ** -0.5

LANES_V7X = 128
SUBLANES_V7X = 8
VMEM_LIMIT_BYTES_V7X = 56 * 1024 * 1024

SB_UNDERFLOW_LOG = 110.0

SB_DONE = 1e30

FOX_UNDERFLOW_LOG2 = 160.0
FOX_BOUND_SLACK = 1.01

ROW_TILE = 256
SB_TILE = 256
SB_CHAINS = 4
FOX_Q_TILE = 1024
FOX_K_TILE = 512


def _cparams(n_axes):
    return pltpu.CompilerParams(
        dimension_semantics=("arbitrary",) * n_axes,
        vmem_limit_bytes=VMEM_LIMIT_BYTES_V7X,
    )


def _resident(block_shape, index_map):
    return pl.BlockSpec(block_shape, index_map, pipeline_mode=pl.Buffered(1))


def _rms_rows(x, gain):
    var = jnp.mean(x * x, axis=-1, keepdims=True)
    return (x * lax.rsqrt(var + EPS)) * gain


def _silu(z):
    return z * (1.0 / (1.0 + jnp.exp(-z)))


def _log_sigmoid(y):
    return jnp.minimum(y, 0.0) - jnp.log(1.0 + jnp.exp(-jnp.abs(y)))


def _dot_nt(a, b):
    return lax.dot_general(a, b, (((1,), (1,)), ((), ())), preferred_element_type=F32)


def _dot(a, b):
    return jnp.dot(a, b, preferred_element_type=F32)


def _in_proj_kernel(d_grp, x_ref, g_ref, w_ref, bf_ref, *refs):
    outs = refs[-13:]
    (qs_ref, ks_ref, vs_ref, ksb_ref, vsb_ref, gs_ref,
     qf_ref, kf_ref, vf_ref, kfb_ref, vfb_ref, gf_ref, lf_ref) = outs
    hb = _rms_rows(x_ref[...], g_ref[...]).astype(BF16)

    def proj(col, width):
        return _dot(hb, w_ref[:, col * d_grp: col * d_grp + width])

    def store_heads(ref, a):
        n_heads = d_grp // HEAD_DIM
        for h in range(n_heads):
            ref[pl.ds(h, a.shape[0], stride=n_heads), :] = a[:, h * HEAD_DIM:(h + 1) * HEAD_DIM]

    qs_ref[...] = (proj(0, d_grp) * ATTN_SCALE).astype(BF16)
    k = proj(1, d_grp)
    store_heads(ks_ref, k)
    ksb_ref[...] = k.astype(BF16)
    v = proj(2, d_grp)
    store_heads(vs_ref, v)
    vsb_ref[...] = v.astype(BF16)
    gs_ref[...] = _silu(proj(3, d_grp))
    qf_ref[...] = (proj(4, d_grp) * (ATTN_SCALE * LOG2E)).astype(BF16)
    k = proj(5, d_grp)
    store_heads(kf_ref, k)
    kfb_ref[...] = k.astype(BF16)
    v = proj(6, d_grp)
    store_heads(vf_ref, v)
    vfb_ref[...] = v.astype(BF16)
    gf_ref[...] = _silu(proj(7, d_grp))
    lf_ref[...] = _log_sigmoid(proj(8, LANES_V7X) + bf_ref[...])


def _in_proj(x, g, w, bf, layer, depth, slabs):
    n, d_model = x.shape
    d_grp = (w.shape[1] - LANES_V7X) // 8
    tm = min(ROW_TILE, n)
    assert n % tm == 0
    row = lambda i: (i, 0)
    fixed = lambda i: (0, 0)
    n_heads = d_grp // HEAD_DIM
    slab_spec = pl.BlockSpec((None, tm * n_heads, HEAD_DIM), lambda i: (layer, i, 0))
    grp_f32 = jax.ShapeDtypeStruct((n, d_grp), F32)
    grp_bf16 = jax.ShapeDtypeStruct((n, d_grp), BF16)
    slab = jax.ShapeDtypeStruct((depth, n * n_heads, HEAD_DIM), F32)
    grp_spec = pl.BlockSpec((tm, d_grp), row)
    out_shape = (grp_bf16, slab, slab, grp_bf16, grp_bf16, grp_f32,
                 grp_bf16, slab, slab, grp_bf16, grp_bf16, grp_f32,
                 jax.ShapeDtypeStruct((n, LANES_V7X), F32))
    out_specs = (grp_spec, slab_spec, slab_spec, grp_spec, grp_spec, grp_spec,
                 grp_spec, slab_spec, slab_spec, grp_spec, grp_spec, grp_spec,
                 pl.BlockSpec((tm, LANES_V7X), row))
    in_specs = [pl.BlockSpec((tm, d_model), row), _resident((1, d_model), fixed),
                _resident(w.shape, fixed), _resident((1, LANES_V7X), fixed)]
    args = [x, g, w, bf]
    aliases = {}
    if slabs is not None:
        in_specs += [pl.BlockSpec(memory_space=pl.ANY)] * 4
        args += list(slabs)
        aliases = {4: 1, 5: 2, 6: 7, 7: 8}
    return pl.pallas_call(
        functools.partial(_in_proj_kernel, d_grp),
        grid=(n // tm,),
        in_specs=in_specs,
        out_specs=out_specs,
        out_shape=out_shape,
        input_output_aliases=aliases,
        compiler_params=_cparams(1),
        name="in_proj",
    )(*args)


def _cumsum_kernel(x_ref, o_ref):
    x = x_ref[...]
    length = x.shape[1]
    lane = lax.broadcasted_iota(jnp.int32, x.shape, 1)
    shift = 1
    while shift < length:
        x = x + jnp.where(lane >= shift, pltpu.roll(x, shift, axis=1), 0.0)
        shift *= 2
    o_ref[...] = x * LOG2E


def _cumsum_lanes(x):
    return pl.pallas_call(
        _cumsum_kernel,
        out_shape=jax.ShapeDtypeStruct(x.shape, F32),
        compiler_params=pltpu.CompilerParams(vmem_limit_bytes=VMEM_LIMIT_BYTES_V7X),
        name="cumsum",
    )(x)


def _suffix_ones(tk):
    p = lax.broadcasted_iota(jnp.int32, (tk, tk), 0)
    c = lax.broadcasted_iota(jnp.int32, (tk, tk), 1)
    return jnp.where(p >= c, 1.0, 0.0).astype(BF16)


def _sb_tiles(qs, kts, vts, runs, tri, mask):
    zs = [_dot_nt(q, kt) for q, kt in zip(qs, kts)]
    sps = [jnp.maximum(z, 0.0) + jnp.log(1.0 + jnp.exp(-jnp.abs(z))) for z in zs]
    if mask is not None:
        sps = [jnp.where(mask, sp, 0.0) for sp in sps]
    cums = [_dot(sp.astype(BF16), tri) for sp in sps]
    ws = [jnp.exp(z - cum - run) for z, cum, run in zip(zs, cums, runs)]
    if mask is not None:
        ws = [jnp.where(mask, w, 0.0) for w in ws]
    pvs = [_dot(w.astype(BF16), vt) for w, vt in zip(ws, vts)]
    return pvs, [run + cum[:, 0:1] for run, cum in zip(runs, cums)]


def _fox_tile(q, kt, vt, ck, cq, m, l, acc, mask):
    u = _dot_nt(q, kt) - ck
    if mask is not None:
        u = jnp.where(mask, u, NEG_BIG)
    m_new = jnp.maximum(m, jnp.max(u, axis=1, keepdims=True) + cq)
    p = jnp.exp2(u - (m_new - cq))
    alpha = jnp.exp2(m - m_new)
    l = alpha * l + jnp.sum(p, axis=1, keepdims=True)
    acc = alpha * acc + _dot(p.astype(BF16), vt)
    return m_new, l, acc


def _head_norm_gate(o, gain, gate):
    ms = jnp.mean(o * o, axis=-1, keepdims=True)
    return ((o * lax.rsqrt(ms + EPS)) * gain * gate).astype(BF16)


def _tile_iotas(tq, tk):
    return (lax.broadcasted_iota(jnp.int32, (tq, tk), 0), lax.broadcasted_iota(jnp.int32, (tq, tk), 1))


def _sb_prompt_kernel(q_ref, k_ref, v_ref, gate_ref, g_ref, o_ref):
    t = k_ref.shape[0]
    tq = min(SB_TILE, t)
    n_ch = q_ref.shape[0] // tq
    first = pl.program_id(2) * n_ch
    tri = _suffix_ones(tq)
    rows, cols = _tile_iotas(tq, tq)
    qs = [q_ref[c * tq:(c + 1) * tq, :] for c in range(n_ch)]

    def kv_tiles(js):
        starts = [pl.multiple_of(j * tq, tq) for j in js]
        return [k_ref[pl.ds(s, tq), :] for s in starts], [v_ref[pl.ds(s, tq), :] for s in starts]

    accs, runs = _sb_tiles(qs, *kv_tiles([first + c for c in range(n_ch)]),
                           [jnp.zeros((tq, 1), F32)] * n_ch, tri, cols < rows)

    def more(carry):
        it, _, _, mins = carry
        live = [jnp.logical_and(first + c - it >= 0, mins[c] < SB_UNDERFLOW_LOG) for c in range(n_ch)]
        return functools.reduce(jnp.logical_or, live)

    def step(carry):
        it, accs, runs, _ = carry
        js = [first + c - it for c in range(n_ch)]
        runs = [jnp.where(j >= 0, run, SB_DONE) for j, run in zip(js, runs)]
        pvs, runs = _sb_tiles(qs, *kv_tiles([jnp.maximum(j, 0) for j in js]), runs, tri, None)
        accs = [acc + pv for acc, pv in zip(accs, pvs)]
        return it + 1, tuple(accs), tuple(runs), tuple(jnp.min(run) for run in runs)

    mins = tuple(jnp.min(run) for run in runs)
    _, accs, _, _ = lax.while_loop(more, step, (jnp.int32(1), tuple(accs), tuple(runs), mins))
    for c in range(n_ch):
        sl = slice(c * tq, (c + 1) * tq)
        o_ref[sl, :] = _head_norm_gate(accs[c], g_ref[...], gate_ref[sl, :])


def _row_to_column(row):
    n = row.shape[1]
    w = min(LANES_V7X, n)
    rows, cols = _tile_iotas(w, w)
    parts = [jnp.sum(jnp.where(rows == cols, row[:, r:r + w], 0.0), axis=1, keepdims=True)
             for r in range(0, n, w)]
    return jnp.concatenate(parts, axis=0)


def _fox_prompt_kernel(q_ref, k_ref, v_ref, c_ref, gate_ref, g_ref, o_ref,
                       sa_ref, sb_ref, mxa_ref, mxb_ref, kn_ref):
    tq = q_ref.shape[0]
    tk = sa_ref.shape[1]
    n_diag = tq // tk
    n_rep = tk // LANES_V7X
    i = pl.program_id(2)

    @pl.when(i == 0)
    def _():
        def chunk(r, best):
            kk = k_ref[pl.ds(pl.multiple_of(r * tq, tq), tq), :].astype(F32)
            return jnp.maximum(best, jnp.sum(kk * kk, axis=1, keepdims=True))
        best = lax.fori_loop(0, k_ref.shape[0] // tq, chunk, jnp.zeros((tq, 1), F32))
        kn_ref[...] = jnp.broadcast_to(jnp.sqrt(jnp.max(best, axis=0, keepdims=True)), kn_ref.shape)

    q = q_ref[...]
    rows, cols = _tile_iotas(tq, tk)
    cq = jnp.broadcast_to(_row_to_column(c_ref[:, pl.ds(pl.multiple_of(i * tq, tq), tq)]), (tq, LANES_V7X))
    ones = jnp.ones((tk, HEAD_DIM), BF16)
    bufs = ((sa_ref, mxa_ref), (sb_ref, mxb_ref))

    def scores(j, buf, mask):
        s_ref, mx_ref = buf
        s = pl.multiple_of(j * tk, tk)
        u = _dot_nt(q, k_ref[pl.ds(s, tk), :]) - c_ref[:, pl.ds(s, tk)]
        if mask is not None:
            u = jnp.where(mask, u, NEG_BIG)
        s_ref[...] = u
        mx_ref[...] = functools.reduce(
            jnp.maximum, [u[:, g * LANES_V7X:(g + 1) * LANES_V7X] for g in range(n_rep)])

    def absorb(j, buf, m, accp):
        s_ref, mx_ref = buf
        s = pl.multiple_of(j * tk, tk)
        m_new = jnp.maximum(m, jnp.max(mx_ref[...], axis=1, keepdims=True) + cq)
        shift = m_new - cq
        alpha = jnp.exp2(m - m_new)
        p = jnp.exp2(s_ref[...] - jnp.concatenate([shift] * n_rep, axis=1)).astype(BF16)
        v1 = jnp.concatenate([v_ref[pl.ds(s, tk), :], ones], axis=1)
        return m_new, jnp.concatenate([alpha, alpha], axis=1) * accp + _dot(p, v1)

    qf = q.astype(F32)
    reach = jnp.sqrt(jnp.sum(qf * qf, axis=1, keepdims=True)) * (FOX_BOUND_SLACK * kn_ref[0:1, :]) + cq

    def live(j, m):
        c_end = jnp.min(c_ref[:, pl.ds(pl.multiple_of(j * tk, tk), tk)], axis=1, keepdims=True)
        return jnp.max(reach - m - c_end) > -FOX_UNDERFLOW_LOG2

    m = jnp.full((tq, LANES_V7X), NEG_BIG, F32)
    accp = jnp.zeros((tq, 2 * HEAD_DIM), F32)
    n_past = i * n_diag
    scores(n_past, bufs[0], cols <= rows)
    for d in range(n_diag):
        if d + 1 < n_diag:
            scores(n_past + d + 1, bufs[(d + 1) % 2], cols + (d + 1) * tk <= rows)
        else:
            scores(jnp.maximum(n_past - 1, 0), bufs[(d + 1) % 2], None)
        m, accp = absorb(n_past + d, bufs[d % 2], m, accp)

    def more(carry):
        t, _, _, go = carry
        return jnp.logical_and(t < n_past // 2, go)

    def pair(carry):
        t, m, accp, _ = carry
        cur = n_past - 1 - 2 * t
        scores(cur - 1, bufs[1], None)
        m, accp = absorb(cur, bufs[0], m, accp)
        go = live(jnp.maximum(cur - 2, 0), m)
        scores(jnp.maximum(cur - 2, 0), bufs[0], None)
        m, accp = absorb(cur - 1, bufs[1], m, accp)
        return t + 1, m, accp, go

    _, m, accp, _ = lax.while_loop(more, pair, (jnp.int32(0), m, accp, live(jnp.maximum(n_past - 1, 0), m)))
    o_ref[...] = _head_norm_gate(accp[:, :HEAD_DIM] / accp[:, HEAD_DIM:], g_ref[...], gate_ref[...])


def _prompt_attention(kernel, q_rows, q, k, v, gate, gain, cum=None, k_rows=None):
    b, t, d_grp = q.shape
    n_heads = d_grp // HEAD_DIM
    tq = min(q_rows, t)
    assert t % tq == 0
    scratch = []
    if k_rows is not None:
        tk = min(k_rows, tq)
        assert tq % (2 * tk) == 0
        scratch = ([pltpu.VMEM((tq, tk), F32)] * 2 + [pltpu.VMEM((tq, LANES_V7X), F32)] * 2
                   + [pltpu.VMEM((SUBLANES_V7X, LANES_V7X), F32)])
    q_spec = pl.BlockSpec((None, tq, HEAD_DIM), lambda bi, h, i: (bi, i, h))
    kv_spec = pl.BlockSpec((None, t, HEAD_DIM), lambda bi, h, i: (bi, 0, h))
    in_specs = [q_spec, kv_spec, kv_spec]
    args = [q, k, v]
    if cum is not None:
        in_specs.append(pl.BlockSpec((None, 1, t), lambda bi, h, i: (bi * n_heads + h, 0, 0)))
        args.append(cum)
    in_specs += [q_spec, pl.BlockSpec((1, HEAD_DIM), lambda bi, h, i: (0, h))]
    args += [gate, gain]
    return pl.pallas_call(
        kernel,
        grid=(b, n_heads, t // tq),
        in_specs=in_specs,
        out_specs=q_spec,
        out_shape=jax.ShapeDtypeStruct((b, t, d_grp), BF16),
        scratch_shapes=scratch,
        compiler_params=_cparams(3),
        name=kernel.__name__.strip("_"),
    )(*args)


def _head_slices(d_grp):
    return [slice(h * HEAD_DIM, (h + 1) * HEAD_DIM) for h in range(d_grp // HEAD_DIM)]


def _cache_rows(ref, start, size, head, n_heads):
    return ref[pl.ds(start * n_heads + head, size, stride=n_heads), :].astype(BF16)


def _sb_sample_kernel(q_ref, kn_ref, vn_ref, kc_ref, vc_ref, gate_ref, g_ref, o_ref):
    tq, tn = q_ref.shape[0], kn_ref.shape[0]
    heads = _head_slices(q_ref.shape[1])
    n_heads = len(heads)
    past = kc_ref.shape[0] // n_heads
    tk = min(SB_TILE, past)
    qs = [q_ref[:, sl] for sl in heads]
    rows, cols = _tile_iotas(tq, tn)
    accs, runs = _sb_tiles(qs, [kn_ref[:, sl] for sl in heads], [vn_ref[:, sl] for sl in heads],
                           [jnp.zeros((tq, 1), F32)] * len(heads), _suffix_ones(tn), cols < rows)
    tri = _suffix_ones(tk)
    for s in range(past - tk, -1, -tk):
        pvs, runs = _sb_tiles(qs, [_cache_rows(kc_ref, s, tk, h, n_heads) for h in range(n_heads)],
                              [_cache_rows(vc_ref, s, tk, h, n_heads) for h in range(n_heads)], runs, tri, None)
        accs = [acc + pv for acc, pv in zip(accs, pvs)]
    for sl, acc in zip(heads, accs):
        o_ref[:, sl] = _head_norm_gate(acc, g_ref[:, sl], gate_ref[:, sl])


def _fox_sample_kernel(q_ref, kn_ref, vn_ref, kc_ref, vc_ref, c_ref, gate_ref, g_ref, o_ref):
    tq, tn = q_ref.shape[0], kn_ref.shape[0]
    heads = _head_slices(q_ref.shape[1])
    n_heads = len(heads)
    past = kc_ref.shape[0] // n_heads
    rows, cols = _tile_iotas(tq, tn)
    for h, sl in enumerate(heads):
        q = q_ref[:, sl]
        ck = c_ref[h:h + 1, past:past + tn]
        cq = jnp.sum(jnp.where(rows == cols, ck, 0.0), axis=1, keepdims=True)
        carry = _fox_tile(q, kn_ref[:, sl], vn_ref[:, sl], ck, cq,
                          jnp.full((tq, 1), NEG_BIG, F32), jnp.zeros((tq, 1), F32),
                          jnp.zeros((tq, HEAD_DIM), F32), cols <= rows)
        m, l, acc = _fox_tile(q, _cache_rows(kc_ref, 0, past, h, n_heads), _cache_rows(vc_ref, 0, past, h, n_heads),
                              c_ref[h:h + 1, 0:past], cq, *carry, None)
        o_ref[:, sl] = _head_norm_gate(acc / l, g_ref[:, sl], gate_ref[:, sl])


def _sample_attention(kernel, layer, q, k_new, v_new, k_cache, v_cache, gate, gain, cum=None):
    s, tq, d_grp = q.shape
    tn = k_new.shape[1]
    n_heads = d_grp // HEAD_DIM
    rows = lambda r: pl.BlockSpec((None, r, d_grp), lambda si: (si, 0, 0))
    cache = pl.BlockSpec((None, None) + k_cache.shape[2:], lambda si: (layer, si, 0, 0))
    in_specs = [rows(tq), rows(tn), rows(tn), cache, cache]
    args = [q, k_new, v_new, k_cache, v_cache]
    if cum is not None:
        in_specs.append(pl.BlockSpec((None, n_heads, cum.shape[2]), lambda si: (si, 0, 0)))
        args.append(cum)
    in_specs += [rows(tq), pl.BlockSpec((1, d_grp), lambda si: (0, 0))]
    args += [gate, gain]
    return pl.pallas_call(
        kernel,
        grid=(s,),
        in_specs=in_specs,
        out_specs=rows(tq),
        out_shape=jax.ShapeDtypeStruct((s, tq, d_grp), BF16),
        compiler_params=_cparams(1),
        name=kernel.__name__.strip("_"),
    )(*args)


def _out_proj_kernel(final, x_ref, osb_ref, ofx_ref, p_ref, wo_ref, wp_ref, gp_ref, wg_ref, gf_ref, y_ref):
    d_sb = osb_ref.shape[1]
    x1 = x_ref[...] + _dot(osb_ref[...], wo_ref[0:d_sb, :]) + _dot(ofx_ref[...], wo_ref[d_sb:, :])
    r = _rms_rows(x1, gp_ref[...]).astype(BF16)
    gate = 1.0 / (1.0 + jnp.exp(-_dot(r, wg_ref[...])))
    x2 = x1 + _dot(p_ref[...].astype(BF16), wp_ref[...]) * gate
    y_ref[...] = _rms_rows(x2, gf_ref[...]) if final else x2


def _out_proj(x, o_sb, o_fx, p, w_out, w_ple, g_ple, w_gate, g_final, final):
    n, d_model = x.shape
    tm = min(ROW_TILE, n)
    assert n % tm == 0
    row = lambda i: (i, 0)
    fixed = lambda i: (0, 0)
    rows = lambda a: pl.BlockSpec((tm, a.shape[1]), row)
    whole = lambda a: _resident(a.shape, fixed)
    return pl.pallas_call(
        functools.partial(_out_proj_kernel, final),
        grid=(n // tm,),
        in_specs=[rows(x), rows(o_sb), rows(o_fx), rows(p), whole(w_out), whole(w_ple), whole(g_ple),
                  whole(w_gate), whole(g_final)],
        out_specs=pl.BlockSpec((tm, d_model), row),
        out_shape=jax.ShapeDtypeStruct((n, d_model), F32),
        compiler_params=_cparams(1),
        name="out_proj",
    )(x, o_sb, o_fx, p, w_out, w_ple, g_ple, w_gate, g_final)


def _pad_rows(a, rows):
    return jnp.pad(a, ((0, 0), (0, rows - a.shape[1]), (0, 0)))


def kernel(x_prompt, x_sample, p_prompt, p_sample, cache_sb_k, cache_sb_v, cache_fox_k, cache_fox_v, cache_fox_logf, w_in, b_forget, g_attn_norm, g_out_sb, g_out_fox, w_out, w_ple, g_ple_norm, w_ple_gate, g_final):
    depth = w_in.shape[0]
    b, t, d_model = x_prompt.shape
    s, ts, _ = x_sample.shape
    past = cache_sb_k.shape[2]
    n_fox = b_forget.shape[1]
    n_sb = cache_sb_k.shape[3]
    d_sb, d_fox = n_sb * HEAD_DIM, n_fox * HEAD_DIM
    assert d_sb == d_fox and w_in.shape[2] == 4 * d_sb + 4 * d_fox + n_fox

    w_in_b = jnp.pad(w_in, ((0, 0), (0, 0), (0, LANES_V7X - n_fox))).astype(BF16)
    b_f = jnp.pad(b_forget, ((0, 0), (0, LANES_V7X - n_fox)))[:, None, :]
    w_out_b, w_ple_b, w_gate_b = w_out.astype(BF16), w_ple.astype(BF16), w_ple_gate.astype(BF16)
    g_fin = g_final[None, :]

    xp = x_prompt.reshape(b * t, d_model)
    xs = x_sample.reshape(s * ts, d_model)
    slabs_p = slabs_s = None
    logf_p, logf_s = [], []
    for i in range(depth):
        g_attn, g_ple = g_attn_norm[i][None, :], g_ple_norm[i][None, :]
        g_sb, g_fx = g_out_sb[i][None, :], g_out_fox[i][None, :]
        final = i == depth - 1

        (qs, ks, vs, ksb, vsb, gs, qf, kf, vf, kfb, vfb, gf, lf) = _in_proj(
            xp, g_attn, w_in_b[i], b_f[i], i, depth, slabs_p)
        slabs_p = (ks, vs, kf, vf)
        logf = lf[:, :n_fox].reshape(b, t, n_fox)
        logf_p.append(logf)
        cum = _cumsum_lanes(jnp.swapaxes(logf, 1, 2).reshape(b * n_fox, t)).reshape(b * n_fox, 1, t)
        b3 = lambda a: a.reshape(b, t, a.shape[-1])
        o_sb = _prompt_attention(_sb_prompt_kernel, SB_TILE * SB_CHAINS, b3(qs), b3(ksb), b3(vsb), b3(gs), g_sb)
        o_fx = _prompt_attention(_fox_prompt_kernel, FOX_Q_TILE, b3(qf), b3(kfb), b3(vfb), b3(gf), g_fx, cum,
                                 FOX_K_TILE)
        xp = _out_proj(xp, o_sb.reshape(b * t, d_sb), o_fx.reshape(b * t, d_fox),
                       p_prompt[i].reshape(b * t, -1), w_out_b[i], w_ple_b[i], g_ple, w_gate_b[i], g_fin, final)

        (qs, ks, vs, ksb, vsb, gs, qf, kf, vf, kfb, vfb, gf, lf) = _in_proj(
            xs, g_attn, w_in_b[i], b_f[i], i, depth, slabs_s)
        slabs_s = (ks, vs, kf, vf)
        logf = lf[:, :n_fox].reshape(s, ts, n_fox)
        logf_s.append(logf)
        s3 = lambda a: a.reshape(s, ts, a.shape[-1])
        flat = lambda a: a.reshape(depth, s, -1, HEAD_DIM)
        new = lambda a: _pad_rows(s3(a), LANES_V7X)
        logf_all = jnp.concatenate([cache_fox_logf[i], logf], axis=1)
        logf_all = jnp.swapaxes(_pad_rows(logf_all, past + LANES_V7X), 1, 2)
        cum = _cumsum_lanes(logf_all.reshape(s * n_fox, past + LANES_V7X)).reshape(s, n_fox, -1)
        o_sb = _sample_attention(_sb_sample_kernel, i, s3(qs), new(ksb), new(vsb),
                                 flat(cache_sb_k), flat(cache_sb_v), s3(gs), g_sb)
        o_fx = _sample_attention(_fox_sample_kernel, i, s3(qf), new(kfb), new(vfb),
                                 flat(cache_fox_k), flat(cache_fox_v), s3(gf), g_fx, cum)
        xs = _out_proj(xs, o_sb.reshape(s * ts, d_sb), o_fx.reshape(s * ts, d_fox),
                       p_sample[i].reshape(s * ts, -1), w_out_b[i], w_ple_b[i], g_ple, w_gate_b[i], g_fin, final)

    heads_p = lambda a, n: a.reshape(depth, b, t, n, HEAD_DIM)
    heads_s = lambda a, n: a.reshape(depth, s, ts, n, HEAD_DIM)
    return (xp.reshape(b, t, d_model), xs.reshape(s, ts, d_model),
            heads_p(slabs_p[0], n_sb), heads_p(slabs_p[1], n_sb),
            heads_p(slabs_p[2], n_fox), heads_p(slabs_p[3], n_fox), jnp.stack(logf_p, axis=0),
            heads_s(slabs_s[0], n_sb), heads_s(slabs_s[1], n_sb),
            heads_s(slabs_s[2], n_fox), heads_s(slabs_s[3], n_fox), jnp.stack(logf_s, axis=0))
```

```python
import functools
import math

import jax
import jax.numpy as jnp
from jax import lax
from jax.experimental import pallas as pl
from jax.experimental.pallas import tpu as pltpu

F32 = jnp.float32
BF16 = jnp.bfloat16

HEAD_DIM = 128
EPS = 1e-6
NEG_BIG = -1e30
LOG2E = math.log2(math.e)
ATTN_SCALE = HEAD_DIM ** -0.5

LANES_V7X = 128
SUBLANES_V7X = 8
VMEM_LIMIT_BYTES_V7X = 56 * 1024 * 1024

SB_UNDERFLOW_LOG = 110.0

SB_DONE = 1e30

FOX_UNDERFLOW_LOG2 = 160.0
FOX_BOUND_SLACK = 1.01

ROW_TILE = 256
SB_TILE = 256
SB_CHAINS = 8
FOX_Q_TILE = 1024
FOX_K_TILE = 512
FOX_CHAINS = 1


def _cparams(n_axes):
    return pltpu.CompilerParams(
        dimension_semantics=("arbitrary",) * n_axes,
        vmem_limit_bytes=VMEM_LIMIT_BYTES_V7X,
    )


def _resident(block_shape, index_map):
    return pl.BlockSpec(block_shape, index_map, pipeline_mode=pl.Buffered(1))


def _rms_rows(x, gain):
    var = jnp.mean(x * x, axis=-1, keepdims=True)
    return (x * lax.rsqrt(var + EPS)) * gain


def _silu(z):
    return z * (1.0 / (1.0 + jnp.exp(-z)))


def _log_sigmoid(y):
    return jnp.minimum(y, 0.0) - jnp.log(1.0 + jnp.exp(-jnp.abs(y)))


def _dot_nt(a, b):
    return lax.dot_general(a, b, (((1,), (1,)), ((), ())), preferred_element_type=F32)


def _dot(a, b):
    return jnp.dot(a, b, preferred_element_type=F32)


def _in_proj_kernel(d_grp, x_ref, g_ref, w_ref, bf_ref, *refs):
    outs = refs[-13:]
    (qs_ref, ks_ref, vs_ref, ksb_ref, vsb_ref, gs_ref,
     qf_ref, kf_ref, vf_ref, kfb_ref, vfb_ref, gf_ref, lf_ref) = outs
    hb = _rms_rows(x_ref[...], g_ref[...]).astype(BF16)

    def proj(col, width):
        return _dot(hb, w_ref[:, col * d_grp: col * d_grp + width])

    def store_heads(ref, a):
        n_heads = d_grp // HEAD_DIM
        for h in range(n_heads):
            ref[pl.ds(h, a.shape[0], stride=n_heads), :] = a[:, h * HEAD_DIM:(h + 1) * HEAD_DIM]

    qs_ref[...] = (proj(0, d_grp) * ATTN_SCALE).astype(BF16)
    k = proj(1, d_grp)
    store_heads(ks_ref, k)
    ksb_ref[...] = k.astype(BF16)
    v = proj(2, d_grp)
    store_heads(vs_ref, v)
    vsb_ref[...] = v.astype(BF16)
    gs_ref[...] = _silu(proj(3, d_grp))
    qf_ref[...] = (proj(4, d_grp) * (ATTN_SCALE * LOG2E)).astype(BF16)
    k = proj(5, d_grp)
    store_heads(kf_ref, k)
    kfb_ref[...] = k.astype(BF16)
    v = proj(6, d_grp)
    store_heads(vf_ref, v)
    vfb_ref[...] = v.astype(BF16)
    gf_ref[...] = _silu(proj(7, d_grp))
    lf_ref[...] = _log_sigmoid(proj(8, LANES_V7X) + bf_ref[...]).T[0:SUBLANES_V7X, :]


def _in_proj(x, g, w, bf, layer, depth, slabs):
    n, d_model = x.shape
    d_grp = (w.shape[1] - LANES_V7X) // 8
    tm = min(ROW_TILE, n)
    assert n % tm == 0
    row = lambda i: (i, 0)
    fixed = lambda i: (0, 0)
    n_heads = d_grp // HEAD_DIM
    slab_spec = pl.BlockSpec((None, tm * n_heads, HEAD_DIM), lambda i: (layer, i, 0))
    grp_f32 = jax.ShapeDtypeStruct((n, d_grp), F32)
    grp_bf16 = jax.ShapeDtypeStruct((n, d_grp), BF16)
    slab = jax.ShapeDtypeStruct((depth, n * n_heads, HEAD_DIM), F32)
    grp_spec = pl.BlockSpec((tm, d_grp), row)
    out_shape = (grp_bf16, slab, slab, grp_bf16, grp_bf16, grp_f32,
                 grp_bf16, slab, slab, grp_bf16, grp_bf16, grp_f32,
                 jax.ShapeDtypeStruct((SUBLANES_V7X, n), F32))
    out_specs = (grp_spec, slab_spec, slab_spec, grp_spec, grp_spec, grp_spec,
                 grp_spec, slab_spec, slab_spec, grp_spec, grp_spec, grp_spec,
                 pl.BlockSpec((SUBLANES_V7X, tm), lambda i: (0, i)))
    in_specs = [pl.BlockSpec((tm, d_model), row), _resident((1, d_model), fixed),
                _resident(w.shape, fixed), _resident((1, LANES_V7X), fixed)]
    args = [x, g, w, bf]
    aliases = {}
    if slabs is not None:
        in_specs += [pl.BlockSpec(memory_space=pl.ANY)] * 4
        args += list(slabs)
        aliases = {4: 1, 5: 2, 6: 7, 7: 8}
    return pl.pallas_call(
        functools.partial(_in_proj_kernel, d_grp),
        grid=(n // tm,),
        in_specs=in_specs,
        out_specs=out_specs,
        out_shape=out_shape,
        input_output_aliases=aliases,
        compiler_params=_cparams(1),
        name="in_proj",
    )(*args)


def _cumsum_kernel(x_ref, o_ref):
    x = x_ref[...]
    length = x.shape[1]
    lane = lax.broadcasted_iota(jnp.int32, x.shape, 1)
    shift = 1
    while shift < length:
        x = x + jnp.where(lane >= shift, pltpu.roll(x, shift, axis=1), 0.0)
        shift *= 2
    o_ref[...] = x * LOG2E


def _cumsum_lanes(x):
    return pl.pallas_call(
        _cumsum_kernel,
        out_shape=jax.ShapeDtypeStruct(x.shape, F32),
        compiler_params=pltpu.CompilerParams(vmem_limit_bytes=VMEM_LIMIT_BYTES_V7X),
        name="cumsum",
    )(x)


def _suffix_ones(tk):
    p = lax.broadcasted_iota(jnp.int32, (tk, tk), 0)
    c = lax.broadcasted_iota(jnp.int32, (tk, tk), 1)
    return jnp.where(p >= c, 1.0, 0.0).astype(BF16)


def _sb_tiles(qs, kts, vts, runs, tri, mask):
    zs = [_dot_nt(q, kt) for q, kt in zip(qs, kts)]
    sps = [jnp.maximum(z, 0.0) + jnp.log(1.0 + jnp.exp(-jnp.abs(z))) for z in zs]
    if mask is not None:
        sps = [jnp.where(mask, sp, 0.0) for sp in sps]
    cums = [_dot(sp.astype(BF16), tri) for sp in sps]
    ws = [jnp.exp(z - cum - run) for z, cum, run in zip(zs, cums, runs)]
    if mask is not None:
        ws = [jnp.where(mask, w, 0.0) for w in ws]
    pvs = [_dot(w.astype(BF16), vt) for w, vt in zip(ws, vts)]
    return pvs, [run + cum[:, 0:1] for run, cum in zip(runs, cums)]


def _fox_tile(q, kt, vt, ck, cq, m, l, acc, mask):
    u = _dot_nt(q, kt) - ck
    if mask is not None:
        u = jnp.where(mask, u, NEG_BIG)
    m_new = jnp.maximum(m, jnp.max(u, axis=1, keepdims=True) + cq)
    p = jnp.exp2(u - (m_new - cq))
    alpha = jnp.exp2(m - m_new)
    l = alpha * l + jnp.sum(p, axis=1, keepdims=True)
    acc = alpha * acc + _dot(p.astype(BF16), vt)
    return m_new, l, acc


def _head_norm_gate(o, gain, gate):
    ms = jnp.mean(o * o, axis=-1, keepdims=True)
    return ((o * lax.rsqrt(ms + EPS)) * gain * gate).astype(BF16)


def _tile_iotas(tq, tk):
    return (lax.broadcasted_iota(jnp.int32, (tq, tk), 0), lax.broadcasted_iota(jnp.int32, (tq, tk), 1))


def _sb_prompt_kernel(q_ref, k_ref, v_ref, gate_ref, g_ref, o_ref):
    t = k_ref.shape[0]
    tq = min(SB_TILE, t)
    n_ch = q_ref.shape[0] // tq
    first = pl.program_id(2) * n_ch
    tri = _suffix_ones(tq)
    rows, cols = _tile_iotas(tq, tq)
    qs = [q_ref[c * tq:(c + 1) * tq, :] for c in range(n_ch)]

    def kv_tiles(js):
        starts = [pl.multiple_of(j * tq, tq) for j in js]
        return [k_ref[pl.ds(s, tq), :] for s in starts], [v_ref[pl.ds(s, tq), :] for s in starts]

    accs, runs = _sb_tiles(qs, *kv_tiles([first + c for c in range(n_ch)]),
                           [jnp.zeros((tq, 1), F32)] * n_ch, tri, cols < rows)

    def more(carry):
        it, _, _, mins = carry
        live = [jnp.logical_and(first + c - it >= 0, mins[c] < SB_UNDERFLOW_LOG) for c in range(n_ch)]
        return functools.reduce(jnp.logical_or, live)

    def step(carry):
        it, accs, runs, _ = carry
        js = [first + c - it for c in range(n_ch)]
        runs = [jnp.where(j >= 0, run, SB_DONE) for j, run in zip(js, runs)]
        pvs, runs = _sb_tiles(qs, *kv_tiles([jnp.maximum(j, 0) for j in js]), runs, tri, None)
        accs = [acc + pv for acc, pv in zip(accs, pvs)]
        return it + 1, tuple(accs), tuple(runs), tuple(jnp.min(run) for run in runs)

    mins = tuple(jnp.min(run) for run in runs)
    _, accs, _, _ = lax.while_loop(more, step, (jnp.int32(1), tuple(accs), tuple(runs), mins))
    for c in range(n_ch):
        sl = slice(c * tq, (c + 1) * tq)
        o_ref[sl, :] = _head_norm_gate(accs[c], g_ref[...], gate_ref[sl, :])


def _row_to_column(row):
    n = row.shape[1]
    w = min(LANES_V7X, n)
    rows, cols = _tile_iotas(w, w)
    parts = [jnp.sum(jnp.where(rows == cols, row[:, r:r + w], 0.0), axis=1, keepdims=True)
             for r in range(0, n, w)]
    return jnp.concatenate(parts, axis=0)


def _fox_prompt_kernel(q_ref, k_ref, v_ref, c_ref, gate_ref, g_ref, o_ref, s_ref, mx_ref, kn_ref):
    n_ch, tq, _ = q_ref.shape
    tk = s_ref.shape[3]
    n_diag = tq // tk
    n_rep = tk // LANES_V7X
    i = pl.program_id(2)
    chains = range(n_ch)

    @pl.when(i == 0)
    def _():
        for c in chains:
            def chunk(r, best):
                kk = k_ref[c, pl.ds(pl.multiple_of(r * tq, tq), tq), :].astype(F32)
                return jnp.maximum(best, jnp.sum(kk * kk, axis=1, keepdims=True))
            best = lax.fori_loop(0, k_ref.shape[1] // tq, chunk, jnp.zeros((tq, 1), F32))
            kn_ref[c] = jnp.broadcast_to(jnp.sqrt(jnp.max(best, axis=0, keepdims=True)), kn_ref.shape[1:])

    qs = [q_ref[c] for c in chains]
    rows, cols = _tile_iotas(tk, tk)
    causal = cols <= rows
    q_start = pl.multiple_of(i * tq, tq)
    cqs = [jnp.broadcast_to(_row_to_column(c_ref[c:c + 1, pl.ds(q_start, tq)]), (tq, LANES_V7X)) for c in chains]
    ones = jnp.ones((tk, HEAD_DIM), BF16)

    def scores(j, slot, diag):
        s = pl.multiple_of(j * tk, tk)
        first = 0 if diag is None else diag * tk
        for c in chains:
            u = _dot_nt(qs[c][first:], k_ref[c, pl.ds(s, tk), :]) - c_ref[c:c + 1, pl.ds(s, tk)]
            if diag is not None:
                top = jnp.where(causal, u[:tk], NEG_BIG)
                u = jnp.concatenate([top, u[tk:]], axis=0) if first + tk < tq else top
            s_ref[c, slot, first:, :] = u
            mx_ref[c, slot, first:, :] = functools.reduce(
                jnp.maximum, [u[:, g * LANES_V7X:(g + 1) * LANES_V7X] for g in range(n_rep)])

    def absorb(j, slot, ms, accps, first=0):
        s = pl.multiple_of(j * tk, tk)
        v1s = [jnp.concatenate([v_ref[c, pl.ds(s, tk), :], ones], axis=1) for c in chains]
        m_olds = [ms[c][first:] for c in chains]
        cq_parts = [cqs[c][first:] for c in chains]
        m_news = [jnp.maximum(m_olds[c], jnp.max(mx_ref[c, slot, first:, :], axis=1, keepdims=True) + cq_parts[c])
                  for c in chains]
        ps = [jnp.exp2(s_ref[c, slot, first:, :]
                       - jnp.concatenate([m_news[c] - cq_parts[c]] * n_rep, axis=1)).astype(BF16) for c in chains]
        alphas = [jnp.exp2(m_olds[c] - m_news[c]) for c in chains]
        accp_news = [jnp.concatenate([alphas[c]] * 2, axis=1) * accps[c][first:] + _dot(ps[c], v1s[c])
                     for c in chains]
        if first:
            m_news = [jnp.concatenate([ms[c][:first], m_news[c]], axis=0) for c in chains]
            accp_news = [jnp.concatenate([accps[c][:first], accp_news[c]], axis=0) for c in chains]
        return m_news, accp_news

    qfs = [q.astype(F32) for q in qs]
    reaches = [jnp.sqrt(jnp.sum(qfs[c] * qfs[c], axis=1, keepdims=True)) * (FOX_BOUND_SLACK * kn_ref[c, 0:1, :])
               + cqs[c] for c in chains]

    def live(j, ms):
        s = pl.multiple_of(j * tk, tk)
        worst = [jnp.max(reaches[c] - ms[c] - jnp.min(c_ref[c:c + 1, pl.ds(s, tk)], axis=1, keepdims=True))
                 for c in chains]
        return functools.reduce(jnp.maximum, worst) > -FOX_UNDERFLOW_LOG2

    ms = [jnp.full((tq, LANES_V7X), NEG_BIG, F32)] * n_ch
    accps = [jnp.zeros((tq, 2 * HEAD_DIM), F32)] * n_ch
    n_past = i * n_diag
    scores(n_past, 0, 0)
    for d in range(n_diag):
        if d + 1 < n_diag:
            scores(n_past + d + 1, (d + 1) % 2, d + 1)
        else:
            scores(jnp.maximum(n_past - 1, 0), (d + 1) % 2, None)
        ms, accps = absorb(n_past + d, d % 2, ms, accps, d * tk)

    def more(carry):
        t, _, _, go = carry
        return jnp.logical_and(t < n_past // 2, go)

    def pair(carry):
        t, ms, accps, _ = carry
        cur = n_past - 1 - 2 * t
        scores(cur - 1, 1, None)
        ms, accps = absorb(cur, 0, ms, accps)
        go = live(jnp.maximum(cur - 2, 0), ms)
        scores(jnp.maximum(cur - 2, 0), 0, None)
        ms, accps = absorb(cur - 1, 1, ms, accps)
        return t + 1, tuple(ms), tuple(accps), go

    init = (jnp.int32(0), tuple(ms), tuple(accps), live(jnp.maximum(n_past - 1, 0), ms))
    _, ms, accps, _ = lax.while_loop(more, pair, init)
    for c in chains:
        o_ref[c] = _head_norm_gate(accps[c][:, :HEAD_DIM] / accps[c][:, HEAD_DIM:], g_ref[...], gate_ref[c])


def _fox_prompt_attention(q, k, v, gate, gain, cum):
    b, t, d_grp = q.shape
    n_heads = d_grp // HEAD_DIM
    tq = min(FOX_Q_TILE, t)
    tk = min(FOX_K_TILE, tq)
    n_ch = FOX_CHAINS
    assert t % tq == 0 and tq % (2 * tk) == 0
    assert b % n_ch == 0
    q_spec = pl.BlockSpec((n_ch, tq, HEAD_DIM), lambda bi, h, i: (bi, i, h))
    kv_spec = pl.BlockSpec((n_ch, t, HEAD_DIM), lambda bi, h, i: (bi, 0, h))
    return pl.pallas_call(
        _fox_prompt_kernel,
        grid=(b // n_ch, n_heads, t // tq),
        in_specs=[q_spec, kv_spec, kv_spec, pl.BlockSpec((None, None, n_ch, t), lambda bi, h, i: (h, bi, 0, 0)),
                  q_spec, pl.BlockSpec((1, HEAD_DIM), lambda bi, h, i: (0, h))],
        out_specs=q_spec,
        out_shape=jax.ShapeDtypeStruct((b, t, d_grp), BF16),
        scratch_shapes=[pltpu.VMEM((n_ch, 2, tq, tk), F32), pltpu.VMEM((n_ch, 2, tq, LANES_V7X), F32),
                        pltpu.VMEM((n_ch, SUBLANES_V7X, LANES_V7X), F32)],
        compiler_params=_cparams(3),
        name="fox_prompt_kernel",
    )(q, k, v, cum.reshape(n_heads, b // n_ch, n_ch, t), gate, gain)


def _sb_prompt_attention(q, k, v, gate, gain):
    b, t, d_grp = q.shape
    n_heads = d_grp // HEAD_DIM
    tq = min(SB_TILE * SB_CHAINS, t)
    assert t % tq == 0
    q_spec = pl.BlockSpec((None, tq, HEAD_DIM), lambda bi, h, i: (bi, i, h))
    kv_spec = pl.BlockSpec((None, t, HEAD_DIM), lambda bi, h, i: (bi, 0, h))
    return pl.pallas_call(
        _sb_prompt_kernel,
        grid=(b, n_heads, t // tq),
        in_specs=[q_spec, kv_spec, kv_spec, q_spec, pl.BlockSpec((1, HEAD_DIM), lambda bi, h, i: (0, h))],
        out_specs=q_spec,
        out_shape=jax.ShapeDtypeStruct((b, t, d_grp), BF16),
        compiler_params=_cparams(3),
        name="sb_prompt_kernel",
    )(q, k, v, gate, gain)


def _head_slices(d_grp):
    return [slice(h * HEAD_DIM, (h + 1) * HEAD_DIM) for h in range(d_grp // HEAD_DIM)]


def _cache_rows(ref, start, size, head, n_heads):
    return ref[pl.ds(start * n_heads + head, size, stride=n_heads), :].astype(BF16)


def _sb_sample_kernel(q_ref, kn_ref, vn_ref, kc_ref, vc_ref, gate_ref, g_ref, o_ref):
    tq, tn = q_ref.shape[0], kn_ref.shape[0]
    heads = _head_slices(q_ref.shape[1])
    n_heads = len(heads)
    past = kc_ref.shape[0] // n_heads
    tk = min(SB_TILE, past)
    qs = [q_ref[:, sl] for sl in heads]
    rows, cols = _tile_iotas(tq, tn)
    accs, runs = _sb_tiles(qs, [kn_ref[:, sl] for sl in heads], [vn_ref[:, sl] for sl in heads],
                           [jnp.zeros((tq, 1), F32)] * len(heads), _suffix_ones(tn), cols < rows)
    tri = _suffix_ones(tk)
    for s in range(past - tk, -1, -tk):
        pvs, runs = _sb_tiles(qs, [_cache_rows(kc_ref, s, tk, h, n_heads) for h in range(n_heads)],
                              [_cache_rows(vc_ref, s, tk, h, n_heads) for h in range(n_heads)], runs, tri, None)
        accs = [acc + pv for acc, pv in zip(accs, pvs)]
    for sl, acc in zip(heads, accs):
        o_ref[:, sl] = _head_norm_gate(acc, g_ref[:, sl], gate_ref[:, sl])


def _fox_sample_kernel(q_ref, kn_ref, vn_ref, kc_ref, vc_ref, c_ref, gate_ref, g_ref, o_ref):
    tq, tn = q_ref.shape[0], kn_ref.shape[0]
    heads = _head_slices(q_ref.shape[1])
    n_heads = len(heads)
    past = kc_ref.shape[0] // n_heads
    rows, cols = _tile_iotas(tq, tn)
    for h, sl in enumerate(heads):
        q = q_ref[:, sl]
        ck = c_ref[h:h + 1, past:past + tn]
        cq = jnp.sum(jnp.where(rows == cols, ck, 0.0), axis=1, keepdims=True)
        carry = _fox_tile(q, kn_ref[:, sl], vn_ref[:, sl], ck, cq,
                          jnp.full((tq, 1), NEG_BIG, F32), jnp.zeros((tq, 1), F32),
                          jnp.zeros((tq, HEAD_DIM), F32), cols <= rows)
        m, l, acc = _fox_tile(q, _cache_rows(kc_ref, 0, past, h, n_heads), _cache_rows(vc_ref, 0, past, h, n_heads),
                              c_ref[h:h + 1, 0:past], cq, *carry, None)
        o_ref[:, sl] = _head_norm_gate(acc / l, g_ref[:, sl], gate_ref[:, sl])


def _sample_attention(kernel, layer, q, k_new, v_new, k_cache, v_cache, gate, gain, cum=None):
    s, tq, d_grp = q.shape
    tn = k_new.shape[1]
    n_heads = d_grp // HEAD_DIM
    rows = lambda r: pl.BlockSpec((None, r, d_grp), lambda si: (si, 0, 0))
    cache = pl.BlockSpec((None, None) + k_cache.shape[2:], lambda si: (layer, si, 0, 0))
    in_specs = [rows(tq), rows(tn), rows(tn), cache, cache]
    args = [q, k_new, v_new, k_cache, v_cache]
    if cum is not None:
        in_specs.append(pl.BlockSpec((None, n_heads, cum.shape[2]), lambda si: (si, 0, 0)))
        args.append(cum)
    in_specs += [rows(tq), pl.BlockSpec((1, d_grp), lambda si: (0, 0))]
    args += [gate, gain]
    return pl.pallas_call(
        kernel,
        grid=(s,),
        in_specs=in_specs,
        out_specs=rows(tq),
        out_shape=jax.ShapeDtypeStruct((s, tq, d_grp), BF16),
        compiler_params=_cparams(1),
        name=kernel.__name__.strip("_"),
    )(*args)


def _out_proj_kernel(final, x_ref, osb_ref, ofx_ref, p_ref, wo_ref, wp_ref, gp_ref, wg_ref, gf_ref, y_ref):
    d_sb = osb_ref.shape[1]
    x1 = x_ref[...] + _dot(osb_ref[...], wo_ref[0:d_sb, :]) + _dot(ofx_ref[...], wo_ref[d_sb:, :])
    r = _rms_rows(x1, gp_ref[...]).astype(BF16)
    gate = 1.0 / (1.0 + jnp.exp(-_dot(r, wg_ref[...])))
    x2 = x1 + _dot(p_ref[...].astype(BF16), wp_ref[...]) * gate
    y_ref[...] = _rms_rows(x2, gf_ref[...]) if final else x2


def _out_proj(x, o_sb, o_fx, p, w_out, w_ple, g_ple, w_gate, g_final, final):
    n, d_model = x.shape
    tm = min(ROW_TILE, n)
    assert n % tm == 0
    row = lambda i: (i, 0)
    fixed = lambda i: (0, 0)
    rows = lambda a: pl.BlockSpec((tm, a.shape[1]), row)
    whole = lambda a: _resident(a.shape, fixed)
    return pl.pallas_call(
        functools.partial(_out_proj_kernel, final),
        grid=(n // tm,),
        in_specs=[rows(x), rows(o_sb), rows(o_fx), rows(p), whole(w_out), whole(w_ple), whole(g_ple),
                  whole(w_gate), whole(g_final)],
        out_specs=pl.BlockSpec((tm, d_model), row),
        out_shape=jax.ShapeDtypeStruct((n, d_model), F32),
        compiler_params=_cparams(1),
        name="out_proj",
    )(x, o_sb, o_fx, p, w_out, w_ple, g_ple, w_gate, g_final)


def _pad_rows(a, rows):
    return jnp.pad(a, ((0, 0), (0, rows - a.shape[1]), (0, 0)))


def kernel(x_prompt, x_sample, p_prompt, p_sample, cache_sb_k, cache_sb_v, cache_fox_k, cache_fox_v, cache_fox_logf, w_in, b_forget, g_attn_norm, g_out_sb, g_out_fox, w_out, w_ple, g_ple_norm, w_ple_gate, g_final):
    depth = w_in.shape[0]
    b, t, d_model = x_prompt.shape
    s, ts, _ = x_sample.shape
    past = cache_sb_k.shape[2]
    n_fox = b_forget.shape[1]
    n_sb = cache_sb_k.shape[3]
    d_sb, d_fox = n_sb * HEAD_DIM, n_fox * HEAD_DIM
    assert d_sb == d_fox and w_in.shape[2] == 4 * d_sb + 4 * d_fox + n_fox

    w_in_b = jnp.pad(w_in, ((0, 0), (0, 0), (0, LANES_V7X - n_fox))).astype(BF16)
    b_f = jnp.pad(b_forget, ((0, 0), (0, LANES_V7X - n_fox)))[:, None, :]
    w_out_b, w_ple_b, w_gate_b = w_out.astype(BF16), w_ple.astype(BF16), w_ple_gate.astype(BF16)
    g_fin = g_final[None, :]

    xp = x_prompt.reshape(b * t, d_model)
    xs = x_sample.reshape(s * ts, d_model)
    slabs_p = slabs_s = None
    logf_p, logf_s = [], []
    for i in range(depth):
        g_attn, g_ple = g_attn_norm[i][None, :], g_ple_norm[i][None, :]
        g_sb, g_fx = g_out_sb[i][None, :], g_out_fox[i][None, :]
        final = i == depth - 1

        (qs, ks, vs, ksb, vsb, gs, qf, kf, vf, kfb, vfb, gf, lf) = _in_proj(
            xp, g_attn, w_in_b[i], b_f[i], i, depth, slabs_p)
        slabs_p = (ks, vs, kf, vf)
        logf_p.append(jnp.transpose(lf[:n_fox].reshape(n_fox, b, t), (1, 2, 0)))
        cum = _cumsum_lanes(lf[:n_fox].reshape(n_fox * b, t)).reshape(n_fox, b, t)
        b3 = lambda a: a.reshape(b, t, a.shape[-1])
        o_sb = _sb_prompt_attention(b3(qs), b3(ksb), b3(vsb), b3(gs), g_sb)
        o_fx = _fox_prompt_attention(b3(qf), b3(kfb), b3(vfb), b3(gf), g_fx, cum)
        xp = _out_proj(xp, o_sb.reshape(b * t, d_sb), o_fx.reshape(b * t, d_fox),
                       p_prompt[i].reshape(b * t, -1), w_out_b[i], w_ple_b[i], g_ple, w_gate_b[i], g_fin, final)

        (qs, ks, vs, ksb, vsb, gs, qf, kf, vf, kfb, vfb, gf, lf) = _in_proj(
            xs, g_attn, w_in_b[i], b_f[i], i, depth, slabs_s)
        slabs_s = (ks, vs, kf, vf)
        logf = jnp.transpose(lf[:n_fox].reshape(n_fox, s, ts), (1, 2, 0))
        logf_s.append(logf)
        s3 = lambda a: a.reshape(s, ts, a.shape[-1])
        flat = lambda a: a.reshape(depth, s, -1, HEAD_DIM)
        new = lambda a: _pad_rows(s3(a), LANES_V7X)
        logf_all = jnp.concatenate([cache_fox_logf[i], logf], axis=1)
        logf_all = jnp.swapaxes(_pad_rows(logf_all, past + LANES_V7X), 1, 2)
        cum = _cumsum_lanes(logf_all.reshape(s * n_fox, past + LANES_V7X)).reshape(s, n_fox, -1)
        o_sb = _sample_attention(_sb_sample_kernel, i, s3(qs), new(ksb), new(vsb),
                                 flat(cache_sb_k), flat(cache_sb_v), s3(gs), g_sb)
        o_fx = _sample_attention(_fox_sample_kernel, i, s3(qf), new(kfb), new(vfb),
                                 flat(cache_fox_k), flat(cache_fox_v), s3(gf), g_fx, cum)
        xs = _out_proj(xs, o_sb.reshape(s * ts, d_sb), o_fx.reshape(s * ts, d_fox),
                       p_sample[i].reshape(s * ts, -1), w_out_b[i], w_ple_b[i], g_ple, w_gate_b[i], g_fin, final)

    heads_p = lambda a, n: a.reshape(depth, b, t, n, HEAD_DIM)
    heads_s = lambda a, n: a.reshape(depth, s, ts, n, HEAD_DIM)
    return (xp.reshape(b, t, d_model), xs.reshape(s, ts, d_model),
            heads_p(slabs_p[0], n_sb), heads_p(slabs_p[1], n_sb),
            heads_p(slabs_p[2], n_fox), heads_p(slabs_p[3], n_fox), jnp.stack(logf_p, axis=0),
            heads_s(slabs_s[0], n_sb), heads_s(slabs_s[1], n_sb),
            heads_s(slabs_s[2], n_fox), heads_s(slabs_s[3], n_fox), jnp.stack(logf_s, axis=0))
```
